```python
import math
import jax, jax.numpy as jnp
from jax import lax
import numpy as np

D_MODEL = 1024
BATCH = 16
SEQ = 2048
DEPTH = 4
DEC_BATCH = 2
DEC_SEQ = 16384
PAST_LEN = 128

GRID_W = 64
Q_BLOCK = 128
N_MIXERS = 3
EPS = 1e-6
ROPE_THETA = 10000.0
LAYER_KINDS = tuple(i % N_MIXERS for i in range(DEPTH))
N_MLA_LAYERS = LAYER_KINDS.count(0)
N_GQA_LAYERS = LAYER_KINDS.count(1)
N_DIFF_LAYERS = LAYER_KINDS.count(2)

MLA_HEADS = 16
MLA_Q_LORA = 256
MLA_KV_LORA = 128
MLA_NOPE = 64
MLA_ROPE = 32
MLA_V = 64
MLA_WIDTH = MLA_HEADS * MLA_V
MLA_IN = MLA_Q_LORA + MLA_KV_LORA + MLA_ROPE + MLA_WIDTH

GQA_HEADS = 8
GQA_KV_HEADS = 2
GQA_HEAD_DIM = 128
GQA_REP = GQA_HEADS // GQA_KV_HEADS
GQA_WIDTH = GQA_HEADS * GQA_HEAD_DIM
GQA_KV_WIDTH = GQA_KV_HEADS * GQA_HEAD_DIM
GQA_IN = GQA_WIDTH + 2 * GQA_KV_WIDTH + GQA_WIDTH

DIFF_HEADS = 8
DIFF_HEAD_DIM = 64
DIFF_WIDTH = DIFF_HEADS * 2 * DIFF_HEAD_DIM
DIFF_IN = 4 * DIFF_WIDTH

REL_BUCKETS = 32
REL_MAX_DIST = 128

kernel_name = "hybrid_mla_gqa_diff_encoder"


def _rmsnorm(x, g):
    xf = x.astype(jnp.float32)
    y = xf * lax.rsqrt(jnp.mean(xf * xf, axis=-1, keepdims=True) + EPS)
    return (y * g.astype(jnp.float32)).astype(x.dtype)


def _rope_angles(pos, dim):
    inv = ROPE_THETA ** (-(jnp.arange(0, dim, 2, dtype=jnp.float32) / dim))
    return pos.astype(jnp.float32)[:, None] * inv[None, :]


def _rotary(x, cos, sin):
    x1, x2 = jnp.split(x, 2, axis=-1)
    c = cos.astype(x.dtype)
    s = sin.astype(x.dtype)
    return jnp.concatenate([x1 * c - x2 * s, x2 * c + x1 * s], axis=-1)


def _blocks(a):
    b, s = a.shape[:2]
    return jnp.moveaxis(a.reshape(b, s // Q_BLOCK, Q_BLOCK, *a.shape[2:]), 1, 0)


def _unblocks(o):
    o = jnp.moveaxis(o, 0, 1)
    return o.reshape(o.shape[0], -1, *o.shape[3:])


def _t5_bucket(rel):
    half = REL_BUCKETS // 2
    max_exact = half // 2
    base = (rel > 0).astype(jnp.int32) * half
    n = jnp.abs(rel)
    nf = jnp.maximum(n, 1).astype(jnp.float32)
    large = max_exact + (jnp.log(nf / max_exact) / math.log(REL_MAX_DIST / max_exact)
                         * (half - max_exact)).astype(jnp.int32)
    large = jnp.minimum(large, half - 1)
    return base + jnp.where(n < max_exact, n, large)


def _mla_mixer(h, w_in, g_q, w_uq, g_kv, w_ukv, w_o):
    b, s, _ = h.shape
    proj = h @ w_in
    q_lat, kv_lat, k_rope, gate = jnp.split(
        proj, [MLA_Q_LORA, MLA_Q_LORA + MLA_KV_LORA, MLA_Q_LORA + MLA_KV_LORA + MLA_ROPE], axis=-1)
    q = (_rmsnorm(q_lat, g_q) @ w_uq).reshape(b, s, MLA_HEADS, MLA_NOPE + MLA_ROPE)
    q_nope, q_rope = q[..., :MLA_NOPE], q[..., MLA_NOPE:]
    kv = (_rmsnorm(kv_lat, g_kv) @ w_ukv).reshape(b, s, MLA_HEADS, MLA_NOPE + MLA_V)
    k_nope, v = kv[..., :MLA_NOPE], kv[..., MLA_NOPE:]
    ang = _rope_angles(jnp.arange(s), MLA_ROPE)
    cos, sin = jnp.cos(ang), jnp.sin(ang)
    q_rope = _rotary(q_rope, cos[:, None, :], sin[:, None, :])
    k_rope = _rotary(k_rope, cos, sin)
    scale = 1.0 / math.sqrt(MLA_NOPE + MLA_ROPE)

    def blk(args):
        qn, qr = args
        sc = (jnp.einsum('bqhd,bkhd->bhqk', qn, k_nope)
              + jnp.einsum('bqhr,bkr->bhqk', qr, k_rope))
        p = jax.nn.softmax(sc.astype(jnp.float32) * scale, axis=-1).astype(v.dtype)
        return jnp.einsum('bhqk,bkhd->bqhd', p, v)

    o = _unblocks(lax.map(blk, (_blocks(q_nope), _blocks(q_rope)))).reshape(b, s, MLA_WIDTH)
    return (o * jax.nn.silu(gate)) @ w_o


def _gqa_mixer(h, w_in, g_q, g_k, w_o):
    b, s, _ = h.shape
    proj = h @ w_in
    q, k, v, gate = jnp.split(
        proj, [GQA_WIDTH, GQA_WIDTH + GQA_KV_WIDTH, GQA_WIDTH + 2 * GQA_KV_WIDTH], axis=-1)
    q = _rmsnorm(q.reshape(b, s, GQA_HEADS, GQA_HEAD_DIM), g_q)
    k = _rmsnorm(k.reshape(b, s, GQA_KV_HEADS, GQA_HEAD_DIM), g_k)
    v = v.reshape(b, s, GQA_KV_HEADS, GQA_HEAD_DIM)
    rows = s // GRID_W
    row = jnp.repeat(jnp.arange(rows), GRID_W)
    col = jnp.tile(jnp.arange(GRID_W), rows)
    ang = jnp.concatenate([_rope_angles(row, GQA_HEAD_DIM // 2),
                           _rope_angles(col, GQA_HEAD_DIM // 2)], axis=-1)
    cos, sin = jnp.cos(ang)[:, None, :], jnp.sin(ang)[:, None, :]
    q = _rotary(q, cos, sin).reshape(b, s, GQA_KV_HEADS, GQA_REP, GQA_HEAD_DIM)
    k = _rotary(k, cos, sin)
    scale = 1.0 / math.sqrt(GQA_HEAD_DIM)

    def blk(qb):
        sc = jnp.einsum('bqgrd,bkgd->bgrqk', qb, k)
        p = jax.nn.softmax(sc.astype(jnp.float32) * scale, axis=-1).astype(v.dtype)
        return jnp.einsum('bgrqk,bkgd->bqgrd', p, v)

    o = _unblocks(lax.map(blk, _blocks(q))).reshape(b, s, GQA_WIDTH)
    return (o * jax.nn.silu(gate)) @ w_o


def _diff_mixer(h, layer_idx, rel_bias, w_in, lam_q1, lam_k1, lam_q2, lam_k2, g_sub, w_o):
    b, s, _ = h.shape
    proj = h @ w_in
    q, k, v, gate = jnp.split(proj, [DIFF_WIDTH, 2 * DIFF_WIDTH, 3 * DIFF_WIDTH], axis=-1)
    q = q.reshape(b, s, DIFF_HEADS, 2, DIFF_HEAD_DIM)
    k = k.reshape(b, s, DIFF_HEADS, 2, DIFF_HEAD_DIM)
    q1, q2 = q[..., 0, :], q[..., 1, :]
    k1, k2 = k[..., 0, :], k[..., 1, :]
    v = v.reshape(b, s, DIFF_HEADS, 2 * DIFF_HEAD_DIM)
    lam_init = 0.8 - 0.6 * math.exp(-0.3 * layer_idx)
    lam = (jnp.exp(jnp.sum(lam_q1.astype(jnp.float32) * lam_k1.astype(jnp.float32)))
           - jnp.exp(jnp.sum(lam_q2.astype(jnp.float32) * lam_k2.astype(jnp.float32)))
           + lam_init)
    scale = 1.0 / math.sqrt(DIFF_HEAD_DIM)
    kpos = jnp.arange(s)
    nb = s // Q_BLOCK

    def blk(args):
        q1b, q2b, start = args
        qpos = start * Q_BLOCK + jnp.arange(Q_BLOCK)
        bias = rel_bias[_t5_bucket(kpos[None, :] - qpos[:, None])]
        bias = jnp.transpose(bias, (2, 0, 1)).astype(jnp.float32)
        s1 = jnp.einsum('bqhd,bkhd->bhqk', q1b, k1).astype(jnp.float32) * scale + bias
        s2 = jnp.einsum('bqhd,bkhd->bhqk', q2b, k2).astype(jnp.float32) * scale + bias
        a = jax.nn.softmax(s1, axis=-1) - lam * jax.nn.softmax(s2, axis=-1)
        return jnp.einsum('bhqk,bkhd->bqhd', a.astype(v.dtype), v)

    o = _unblocks(lax.map(blk, (_blocks(q1), _blocks(q2), jnp.arange(nb))))
    o = (_rmsnorm(o, g_sub) * (1.0 - lam_init)).reshape(b, s, DIFF_WIDTH)
    return (o * jax.nn.silu(gate)) @ w_o


def _trunk(x, norm_pre, norm_post, rel_bias,
           mla_w_in, mla_g_q, mla_w_uq, mla_g_kv, mla_w_ukv, mla_w_o,
           gqa_w_in, gqa_g_q, gqa_g_k, gqa_w_o,
           dif_w_in, dif_lam_q1, dif_lam_k1, dif_lam_q2, dif_lam_k2, dif_g_sub, dif_w_o):
    ia = ib = ic = 0
    for layer in range(DEPTH):
        kind = LAYER_KINDS[layer]
        h = _rmsnorm(x, norm_pre[layer])
        if kind == 0:
            m = _mla_mixer(h, mla_w_in[ia], mla_g_q[ia], mla_w_uq[ia], mla_g_kv[ia],
                           mla_w_ukv[ia], mla_w_o[ia])
            ia += 1
        elif kind == 1:
            m = _gqa_mixer(h, gqa_w_in[ib], gqa_g_q[ib], gqa_g_k[ib], gqa_w_o[ib])
            ib += 1
        else:
            m = _diff_mixer(h, layer, rel_bias, dif_w_in[ic], dif_lam_q1[ic], dif_lam_k1[ic],
                            dif_lam_q2[ic], dif_lam_k2[ic], dif_g_sub[ic], dif_w_o[ic])
            ic += 1
        x = x + _rmsnorm(m, norm_post[layer])
    return x


def setup_inputs(seed: int = 0) -> dict:
    key = jax.random.key(seed)
    ks = jax.random.split(key, 24)
    f32 = jnp.float32

    def w(k, shape, fan_in):
        return jax.random.normal(k, shape, f32) * (fan_in ** -0.5)

    def gain(k, shape):
        return 1.0 + 0.02 * jax.random.normal(k, shape, f32)

    NA, NB, NC = N_MLA_LAYERS, N_GQA_LAYERS, N_DIFF_LAYERS
    return {
        "x_prompt": jax.random.normal(ks[0], (BATCH, SEQ, D_MODEL), f32),
        "x_sample": jax.random.normal(ks[1], (DEC_BATCH, DEC_SEQ, D_MODEL), f32),
        "norm_pre": gain(ks[2], (DEPTH, D_MODEL)),
        "norm_post": gain(ks[3], (DEPTH, D_MODEL)),
        "rel_bias": 0.5 * jax.random.normal(ks[4], (REL_BUCKETS, DIFF_HEADS), f32),
        "mla_w_in": w(ks[5], (NA, D_MODEL, MLA_IN), D_MODEL),
        "mla_g_q": gain(ks[6], (NA, MLA_Q_LORA)),
        "mla_w_uq": w(ks[7], (NA, MLA_Q_LORA, MLA_HEADS * (MLA_NOPE + MLA_ROPE)), MLA_Q_LORA),
        "mla_g_kv": gain(ks[8], (NA, MLA_KV_LORA)),
        "mla_w_ukv": w(ks[9], (NA, MLA_KV_LORA, MLA_HEADS * (MLA_NOPE + MLA_V)), MLA_KV_LORA),
        "mla_w_o": w(ks[10], (NA, MLA_WIDTH, D_MODEL), MLA_WIDTH),
        "gqa_w_in": w(ks[11], (NB, D_MODEL, GQA_IN), D_MODEL),
        "gqa_g_q": gain(ks[12], (NB, GQA_HEAD_DIM)),
        "gqa_g_k": gain(ks[13], (NB, GQA_HEAD_DIM)),
        "gqa_w_o": w(ks[14], (NB, GQA_WIDTH, D_MODEL), GQA_WIDTH),
        "dif_w_in": w(ks[15], (NC, D_MODEL, DIFF_IN), D_MODEL),
        "dif_lam_q1": 0.1 * jax.random.normal(ks[16], (NC, DIFF_HEAD_DIM), f32),
        "dif_lam_k1": 0.1 * jax.random.normal(ks[17], (NC, DIFF_HEAD_DIM), f32),
        "dif_lam_q2": 0.1 * jax.random.normal(ks[18], (NC, DIFF_HEAD_DIM), f32),
        "dif_lam_k2": 0.1 * jax.random.normal(ks[19], (NC, DIFF_HEAD_DIM), f32),
        "dif_g_sub": gain(ks[20], (NC, 2 * DIFF_HEAD_DIM)),
        "dif_w_o": w(ks[21], (NC, DIFF_WIDTH, D_MODEL), DIFF_WIDTH),
    }


def reference(x_prompt, x_sample, norm_pre, norm_post, rel_bias,
              mla_w_in, mla_g_q, mla_w_uq, mla_g_kv, mla_w_ukv, mla_w_o,
              gqa_w_in, gqa_g_q, gqa_g_k, gqa_w_o,
              dif_w_in, dif_lam_q1, dif_lam_k1, dif_lam_q2, dif_lam_k2, dif_g_sub, dif_w_o):
    params = (norm_pre, norm_post, rel_bias,
              mla_w_in, mla_g_q, mla_w_uq, mla_g_kv, mla_w_ukv, mla_w_o,
              gqa_w_in, gqa_g_q, gqa_g_k, gqa_w_o,
              dif_w_in, dif_lam_q1, dif_lam_k1, dif_lam_q2, dif_lam_k2, dif_g_sub, dif_w_o)
    y_prompt = _trunk(x_prompt, *params)
    y_sample = _trunk(x_sample, *params)
    return (y_prompt, y_sample)
```

```python
import functools
import math

import jax
import jax.numpy as jnp
from jax import lax
from jax.experimental import pallas as pl
from jax.experimental.pallas import tpu as pltpu

D_MODEL = 1024
DEPTH = 4
N_MIXERS = 3
EPS = 1e-6
ROPE_THETA = 10000.0
GRID_W = 64

MLA_HEADS = 16
MLA_Q_LORA = 256
MLA_KV_LORA = 128
MLA_NOPE = 64
MLA_ROPE = 32
MLA_V = 64
MLA_WIDTH = MLA_HEADS * MLA_V

GQA_HEADS = 8
GQA_KV_HEADS = 2
GQA_HEAD_DIM = 128
GQA_REP = GQA_HEADS // GQA_KV_HEADS
GQA_WIDTH = GQA_HEADS * GQA_HEAD_DIM
GQA_KV_WIDTH = GQA_KV_HEADS * GQA_HEAD_DIM

DIFF_HEADS = 8
DIFF_HEAD_DIM = 64
DIFF_WIDTH = DIFF_HEADS * 2 * DIFF_HEAD_DIM

REL_BUCKETS = 32
REL_MAX_DIST = 128

LANES = 128
BF16_SUBLANES = 16
TOKEN_TILE = 512
VMEM_LIMIT = 56 * 1024 * 1024

F32 = jnp.float32
BF16 = jnp.bfloat16


def _cparams(semantics):
    return pltpu.CompilerParams(dimension_semantics=semantics, vmem_limit_bytes=VMEM_LIMIT)


def _rms(x, g):
    return x * lax.rsqrt(jnp.mean(x * x, axis=-1, keepdims=True) + EPS) * g


def _dot(a, b):
    return jnp.dot(a, b, preferred_element_type=F32)


def _ones_row_block(width):
    row = lax.broadcasted_iota(jnp.int32, (BF16_SUBLANES, width), 0)
    return jnp.where(row == 0, 1.0, 0.0).astype(BF16)


def _silu(x):
    return x * (1.0 / (1.0 + jnp.exp(-x)))


def _mla_prep_kernel(x_ref, gpre_ref, w1_ref, gq_ref, wqa_ref, wqb_ref, gkv_ref, wkk_ref, wkv_ref,
                     cos_ref, sin_ref, qt_ref, k_ref, vt_ref, sg_ref):
    tm = x_ref.shape[1]
    h = _rms(x_ref[0], gpre_ref[...]).astype(BF16)
    proj = _dot(h, w1_ref[...])
    o = 0
    q_lat = proj[:, o:o + MLA_Q_LORA]; o += MLA_Q_LORA
    kv_lat = proj[:, o:o + MLA_KV_LORA]; o += MLA_KV_LORA
    kr_a = proj[:, o:o + LANES]; o += LANES
    kr_b = proj[:, o:o + LANES]; o += LANES
    gate = proj[:, o:o + MLA_WIDTH]
    sg_ref[0] = _silu(gate)

    cos = cos_ref[...]
    sin = sin_ref[...]
    scale = 1.0 / math.sqrt(MLA_NOPE + MLA_ROPE)
    qn = _rms(q_lat, gq_ref[...]).astype(BF16)
    qa = _dot(qn, wqa_ref[...])
    qb = _dot(qn, wqb_ref[...])
    kvn = _rms(kv_lat, gkv_ref[...]).astype(BF16)
    kk = _dot(kvn, wkk_ref[...])
    vv = _dot(kvn, wkv_ref[...])
    k_rot = kr_a * cos + kr_b * sin
    ones_blk = _ones_row_block(tm)
    for hd in range(MLA_HEADS):
        sl = slice(hd * LANES, (hd + 1) * LANES)
        qh = (qa[:, sl] * cos + qb[:, sl] * sin) * scale
        qt_ref[0, hd] = qh.T.astype(BF16)
        k_ref[0, hd] = (kk[:, sl] + k_rot).astype(BF16)
        vt_ref[0, hd, MLA_V:MLA_V + BF16_SUBLANES, :] = ones_blk
    for pr in range(MLA_HEADS // 2):
        vt = vv[:, pr * LANES:(pr + 1) * LANES].T.astype(BF16)
        vt_ref[0, 2 * pr, 0:MLA_V, :] = vt[0:MLA_V]
        vt_ref[0, 2 * pr + 1, 0:MLA_V, :] = vt[MLA_V:2 * MLA_V]


def _gqa_prep_kernel(x_ref, gpre_ref, w_ref, gq_ref, gk_ref, cos_ref, sin_ref,
                     qt_ref, k_ref, vt_ref, sg_ref):
    tm = x_ref.shape[1]
    h = _rms(x_ref[0], gpre_ref[...]).astype(BF16)
    proj = _dot(h, w_ref[...])
    cos = cos_ref[...]
    sin = sin_ref[...]
    scale = 1.0 / math.sqrt(GQA_HEAD_DIM)

    def rot(xh, g):
        xn = _rms(xh, g)
        return xn * cos + pltpu.roll(xn, GQA_HEAD_DIM // 2, axis=1) * sin

    for hd in range(GQA_HEADS):
        qh = rot(proj[:, hd * LANES:(hd + 1) * LANES], gq_ref[...]) * scale
        qt_ref[0, hd] = qh.T.astype(BF16)
    ones_blk = _ones_row_block(tm)
    for g in range(GQA_KV_HEADS):
        ko = GQA_WIDTH + g * LANES
        vo = GQA_WIDTH + GQA_KV_WIDTH + g * LANES
        k_ref[0, g] = rot(proj[:, ko:ko + LANES], gk_ref[...]).astype(BF16)
        vt_ref[0, g, 0:GQA_HEAD_DIM, :] = proj[:, vo:vo + LANES].T.astype(BF16)
        vt_ref[0, g, GQA_HEAD_DIM:GQA_HEAD_DIM + BF16_SUBLANES, :] = ones_blk
    go = GQA_WIDTH + 2 * GQA_KV_WIDTH
    sg_ref[0] = _silu(proj[:, go:go + GQA_WIDTH])


def _diff_prep_kernel(x_ref, gpre_ref, w_ref, qt_ref, k_ref, vt_ref, sg_ref):
    tm = x_ref.shape[1]
    h = _rms(x_ref[0], gpre_ref[...]).astype(BF16)
    proj = _dot(h, w_ref[...])
    scale = 1.0 / math.sqrt(DIFF_HEAD_DIM)
    row = lax.broadcasted_iota(jnp.int32, (LANES, tm), 0)
    ones_blk = _ones_row_block(tm)
    for hd in range(DIFF_HEADS):
        sl = slice(hd * LANES, (hd + 1) * LANES)
        qt = (proj[:, sl] * scale).T
        qt_ref[0, hd, 0] = jnp.where(row < DIFF_HEAD_DIM, qt, 0.0).astype(BF16)
        qt_ref[0, hd, 1] = jnp.where(row >= DIFF_HEAD_DIM, qt, 0.0).astype(BF16)
        k_ref[0, hd] = proj[:, DIFF_WIDTH + hd * LANES:DIFF_WIDTH + (hd + 1) * LANES].astype(BF16)
        vo = 2 * DIFF_WIDTH + hd * LANES
        vt_ref[0, hd, 0:LANES, :] = proj[:, vo:vo + LANES].T.astype(BF16)
        vt_ref[0, hd, LANES:LANES + BF16_SUBLANES, :] = ones_blk
    sg_ref[0] = _silu(proj[:, 3 * DIFF_WIDTH:4 * DIFF_WIDTH])


def _full(shape):
    return pl.BlockSpec(shape, lambda b, t: (0,) * len(shape))


def _prep_call(kernel, name, x, consts, tables, n_heads, n_kv, qt_inner, dv, sg_width):
    bsz, seq, _ = x.shape
    tm = TOKEN_TILE
    in_specs = [pl.BlockSpec((1, tm, D_MODEL), lambda b, t: (b, t, 0))]
    in_specs += [_full(c.shape) for c in consts]
    in_specs += [pl.BlockSpec((tm, LANES), lambda b, t: (t, 0)) for _ in tables]
    qt_shape = (bsz, n_heads) + qt_inner + (LANES, seq)
    qt_block = (1, n_heads) + qt_inner + (LANES, tm)
    nq = len(qt_shape)
    out_shape = (
        jax.ShapeDtypeStruct(qt_shape, BF16),
        jax.ShapeDtypeStruct((bsz, n_kv, seq, LANES), BF16),
        jax.ShapeDtypeStruct((bsz, n_kv, dv + BF16_SUBLANES, seq), BF16),
        jax.ShapeDtypeStruct((bsz, seq, sg_width), F32),
    )
    out_specs = (
        pl.BlockSpec(qt_block, lambda b, t: (b,) + (0,) * (nq - 2) + (t,)),
        pl.BlockSpec((1, n_kv, tm, LANES), lambda b, t: (b, 0, t, 0)),
        pl.BlockSpec((1, n_kv, dv + BF16_SUBLANES, tm), lambda b, t: (b, 0, 0, t)),
        pl.BlockSpec((1, tm, sg_width), lambda b, t: (b, t, 0)),
    )
    return pl.pallas_call(
        kernel, grid=(bsz, seq // tm), in_specs=in_specs, out_specs=out_specs, out_shape=out_shape,
        compiler_params=_cparams(("parallel", "parallel")), name=name,
    )(x, *consts, *tables)


def _softmax_step(s, vt, m_ref, acc_ref, idx):
    m_prev = m_ref[idx]
    m_new = jnp.maximum(m_prev, jnp.max(s, axis=0, keepdims=True))
    p = jnp.exp(s - m_new).astype(BF16)
    alpha = jnp.exp(m_prev - m_new)
    acc_ref[idx] = acc_ref[idx] * alpha + _dot(vt, p)
    m_ref[idx] = m_new


def _attn_kernel(qt_ref, k_ref, vt_ref, o_ref, m_ref, acc_ref, *, heads, dv, tk):
    seq = k_ref.shape[2]
    shared_kv = k_ref.shape[1] == 1
    m_ref[...] = jnp.full(m_ref.shape, -jnp.inf, F32)
    acc_ref[...] = jnp.zeros(acc_ref.shape, F32)

    def step(j, carry):
        off = pl.multiple_of(j * tk, tk)
        for hd in range(heads):
            kv = 0 if shared_kv else hd
            s = _dot(k_ref[0, kv, pl.ds(off, tk), :], qt_ref[0, hd])
            _softmax_step(s, vt_ref[0, kv, :, pl.ds(off, tk)], m_ref, acc_ref, hd)
        return carry

    lax.fori_loop(0, seq // tk, step, 0)
    outs = []
    for hd in range(heads):
        acc = acc_ref[hd]
        outs.append(acc[0:dv] * (1.0 / acc[dv:dv + 1]))
    o_t = outs[0] if heads == 1 else jnp.concatenate(outs, axis=0)
    for c in range(heads * dv // LANES):
        o_ref[0, c] = o_t[c * LANES:(c + 1) * LANES].T


def _attention(qt, k, vt, *, heads_per_step, dv, name):
    bsz, n_heads, _, seq = qt.shape
    n_kv = k.shape[1]
    tq = tk = TOKEN_TILE
    hps = heads_per_step
    rep = n_heads // n_kv
    kv_per_step = 1 if rep > 1 else hps
    assert (rep == 1) or (rep % hps == 0)
    kv_index = (lambda h: (h * hps) // rep) if rep > 1 else (lambda h: h)
    cols = hps * dv // LANES
    dvp = dv + BF16_SUBLANES
    kernel = functools.partial(_attn_kernel, heads=hps, dv=dv, tk=tk)
    return pl.pallas_call(
        kernel,
        grid=(bsz, n_heads // hps, seq // tq),
        in_specs=[
            pl.BlockSpec((1, hps, LANES, tq), lambda b, h, q: (b, h, 0, q)),
            pl.BlockSpec((1, kv_per_step, seq, LANES), lambda b, h, q: (b, kv_index(h), 0, 0)),
            pl.BlockSpec((1, kv_per_step, dvp, seq), lambda b, h, q: (b, kv_index(h), 0, 0)),
        ],
        out_specs=pl.BlockSpec((1, cols, tq, LANES), lambda b, h, q: (b, h, q, 0)),
        out_shape=jax.ShapeDtypeStruct((bsz, n_heads * dv // LANES, seq, LANES), F32),
        scratch_shapes=[pltpu.VMEM((hps, 1, tq), F32), pltpu.VMEM((hps, dvp, tq), F32)],
        compiler_params=_cparams(("parallel", "parallel", "arbitrary")), name=name,
    )(qt, k, vt)


def _diff_attn_kernel(lam_ref, gsub_ref, qt_ref, k_ref, vt_ref, bias_ref, o_ref, m_ref, acc_ref,
                      *, tk, lam_init):
    seq = k_ref.shape[2]
    qi = pl.program_id(2)
    m_ref[...] = jnp.full(m_ref.shape, -jnp.inf, F32)
    acc_ref[...] = jnp.zeros(acc_ref.shape, F32)

    def step(j, carry):
        off = pl.multiple_of(j * tk, tk)
        kj = k_ref[0, 0, pl.ds(off, tk), :]
        vt = vt_ref[0, 0, :, pl.ds(off, tk)]
        bias = bias_ref[0, jnp.clip(j - qi, -2, 2) + 2]
        for c in range(2):
            s = _dot(kj, qt_ref[0, 0, c]) + bias
            _softmax_step(s, vt, m_ref, acc_ref, c)
        return carry

    lax.fori_loop(0, seq // tk, step, 0)
    lq1, lk1, lq2, lk2 = (lam_ref[i:i + 1, :] for i in range(4))
    lam = (jnp.exp(jnp.sum(lq1 * lk1, axis=1, keepdims=True))
           - jnp.exp(jnp.sum(lq2 * lk2, axis=1, keepdims=True)) + lam_init)
    a1 = acc_ref[0]
    a2 = acc_ref[1]
    o_t = a1[0:LANES] * (1.0 / a1[LANES:LANES + 1]) - lam * (a2[0:LANES] * (1.0 / a2[LANES:LANES + 1]))
    ms = jnp.mean(o_t * o_t, axis=0, keepdims=True)
    o_t = o_t * lax.rsqrt(ms + EPS) * gsub_ref[...] * (1.0 - lam_init)
    o_ref[0, 0] = o_t.T


def _diff_attention(qt, k, vt, bias, lam_vecs, g_sub_col, *, lam_init, name):
    bsz, n_heads, _, _, seq = qt.shape
    tq = tk = TOKEN_TILE
    dvp = LANES + BF16_SUBLANES
    kernel = functools.partial(_diff_attn_kernel, tk=tk, lam_init=lam_init)
    return pl.pallas_call(
        kernel,
        grid=(bsz, n_heads, seq // tq),
        in_specs=[
            pl.BlockSpec((4, DIFF_HEAD_DIM), lambda b, h, q: (0, 0)),
            pl.BlockSpec((LANES, 1), lambda b, h, q: (0, 0)),
            pl.BlockSpec((1, 1, 2, LANES, tq), lambda b, h, q: (b, h, 0, 0, q)),
            pl.BlockSpec((1, 1, seq, LANES), lambda b, h, q: (b, h, 0, 0)),
            pl.BlockSpec((1, 1, dvp, seq), lambda b, h, q: (b, h, 0, 0)),
            pl.BlockSpec((1, 5, tk, tq), lambda b, h, q: (h, 0, 0, 0)),
        ],
        out_specs=pl.BlockSpec((1, 1, tq, LANES), lambda b, h, q: (b, h, q, 0)),
        out_shape=jax.ShapeDtypeStruct((bsz, n_heads, seq, LANES), F32),
        scratch_shapes=[pltpu.VMEM((2, 1, tq), F32), pltpu.VMEM((2, dvp, tq), F32)],
        compiler_params=_cparams(("parallel", "parallel", "arbitrary")), name=name,
    )(lam_vecs, g_sub_col, qt, k, vt, bias)


def _bias_kernel(rel_ref, bucket_ref, o_ref):
    hd = pl.program_id(0)
    n_tiles, t, _ = bucket_ref.shape
    chunk = 64
    per_tile = t // chunk

    def body(c, carry):
        d = c // per_tile
        rows = pl.ds(pl.multiple_of((c % per_tile) * chunk, chunk), chunk)
        idx = bucket_ref[d, rows, :]
        acc = jnp.zeros(idx.shape, F32)
        for b in range(REL_BUCKETS):
            acc = jnp.where(idx == b, rel_ref[b, hd], acc)
        o_ref[0, d, rows, :] = acc
        return carry

    lax.fori_loop(0, n_tiles * per_tile, body, 0)


def _t5_bucket(rel):
    half = REL_BUCKETS // 2
    max_exact = half // 2
    base = (rel > 0).astype(jnp.int32) * half
    n = jnp.abs(rel)
    nf = jnp.maximum(n, 1).astype(F32)
    large = max_exact + (jnp.log(nf / max_exact) / math.log(REL_MAX_DIST / max_exact)
                         * (half - max_exact)).astype(jnp.int32)
    large = jnp.minimum(large, half - 1)
    return base + jnp.where(n < max_exact, n, large)


def _bias_tiles(rel_bias):
    t = TOKEN_TILE
    assert t > REL_MAX_DIST
    kk = jnp.arange(t)[:, None]
    qq = jnp.arange(t)[None, :]
    buckets = jnp.stack([_t5_bucket(d * t + kk - qq) for d in range(-2, 3)])
    return pl.pallas_call(
        _bias_kernel,
        grid=(DIFF_HEADS,),
        in_specs=[pl.BlockSpec(memory_space=pltpu.SMEM),
                  pl.BlockSpec((5, t, t), lambda h: (0, 0, 0))],
        out_specs=pl.BlockSpec((1, 5, t, t), lambda h: (h, 0, 0, 0)),
        out_shape=jax.ShapeDtypeStruct((DIFF_HEADS, 5, t, t), F32),
        compiler_params=_cparams(("parallel",)), name="t5_bias_tiles",
    )(rel_bias, buckets)


def _out_kernel(x_ref, o_ref, sg_ref, w_ref, g_ref, y_ref):
    cols = o_ref.shape[1]
    gated = [(o_ref[0, c] * sg_ref[0, :, c * LANES:(c + 1) * LANES]).astype(BF16) for c in range(cols)]
    m = _dot(jnp.concatenate(gated, axis=1), w_ref[...])
    y_ref[0] = x_ref[0] + _rms(m, g_ref[...])


def _out_call(x, o, sg, w_o, g_post, name):
    bsz, seq, _ = x.shape
    tm = TOKEN_TILE
    cols = o.shape[1]
    return pl.pallas_call(
        _out_kernel,
        grid=(bsz, seq // tm),
        in_specs=[
            pl.BlockSpec((1, tm, D_MODEL), lambda b, t: (b, t, 0)),
            pl.BlockSpec((1, cols, tm, LANES), lambda b, t: (b, 0, t, 0)),
            pl.BlockSpec((1, tm, cols * LANES), lambda b, t: (b, t, 0)),
            _full(w_o.shape),
            _full(g_post.shape),
        ],
        out_specs=pl.BlockSpec((1, tm, D_MODEL), lambda b, t: (b, t, 0)),
        out_shape=jax.ShapeDtypeStruct(x.shape, F32),
        compiler_params=_cparams(("parallel", "parallel")), name=name,
    )(x, o, sg, w_o, g_post)


def _rope_angles(pos, dim):
    inv = ROPE_THETA ** (-(jnp.arange(0, dim, 2, dtype=F32) / dim))
    return pos.astype(F32)[:, None] * inv[None, :]


def _mla_tables(seq):
    ang = _rope_angles(jnp.arange(seq), MLA_ROPE)
    cos, sin = jnp.cos(ang), jnp.sin(ang)
    pad = LANES - MLA_NOPE - MLA_ROPE
    cos_t = jnp.concatenate([jnp.ones((seq, MLA_NOPE), F32), cos, cos, jnp.zeros((seq, pad), F32)], axis=1)
    sin_t = jnp.concatenate([jnp.zeros((seq, MLA_NOPE), F32), -sin, sin, jnp.zeros((seq, pad), F32)], axis=1)
    return cos_t, sin_t


def _gqa_tables(seq):
    rows = seq // GRID_W
    row = jnp.repeat(jnp.arange(rows), GRID_W)
    col = jnp.tile(jnp.arange(GRID_W), rows)
    ang = jnp.concatenate([_rope_angles(row, GQA_HEAD_DIM // 2), _rope_angles(col, GQA_HEAD_DIM // 2)], axis=-1)
    cos, sin = jnp.cos(ang), jnp.sin(ang)
    return jnp.concatenate([cos, cos], axis=1), jnp.concatenate([-sin, sin], axis=1)


def _mla_weights(w_in, w_uq, w_ukv):
    half = MLA_ROPE // 2
    pad = LANES - MLA_NOPE - MLA_ROPE
    o_kv = MLA_Q_LORA
    o_kr = o_kv + MLA_KV_LORA
    o_gate = o_kr + MLA_ROPE
    w_kr = w_in[:, o_kr:o_gate]
    z = lambda n, rows: jnp.zeros((rows, n), F32)
    kr_a = jnp.concatenate([z(MLA_NOPE, D_MODEL), w_kr, z(pad, D_MODEL)], axis=1)
    kr_b = jnp.concatenate([z(MLA_NOPE, D_MODEL), w_kr[:, half:], w_kr[:, :half], z(pad, D_MODEL)], axis=1)
    w1 = jnp.concatenate([w_in[:, :o_kr], kr_a, kr_b, w_in[:, o_gate:]], axis=1)
    uq = w_uq.reshape(MLA_Q_LORA, MLA_HEADS, MLA_NOPE + MLA_ROPE)
    nope, r1, r2 = uq[..., :MLA_NOPE], uq[..., MLA_NOPE:MLA_NOPE + half], uq[..., MLA_NOPE + half:]
    zq = jnp.zeros((MLA_Q_LORA, MLA_HEADS, pad), F32)
    wq_a = jnp.concatenate([nope, r1, r2, zq], axis=-1).reshape(MLA_Q_LORA, MLA_HEADS * LANES)
    wq_b = jnp.concatenate([jnp.zeros_like(nope), r2, r1, zq], axis=-1).reshape(MLA_Q_LORA, MLA_HEADS * LANES)
    ukv = w_ukv.reshape(MLA_KV_LORA, MLA_HEADS, MLA_NOPE + MLA_V)
    wk = jnp.concatenate([ukv[..., :MLA_NOPE], jnp.zeros((MLA_KV_LORA, MLA_HEADS, LANES - MLA_NOPE), F32)],
                         axis=-1).reshape(MLA_KV_LORA, MLA_HEADS * LANES)
    wv = ukv[..., MLA_NOPE:].reshape(MLA_KV_LORA, MLA_WIDTH)
    return tuple(w.astype(BF16) for w in (w1, wq_a, wq_b, wk, wv))


def _trunk(x, tag, norm_pre, norm_post, bias_tiles, mla, gqa, dif):
    bsz, seq, _ = x.shape
    ia = ib = ic = 0
    for layer in range(DEPTH):
        kind = layer % N_MIXERS
        g_pre = norm_pre[layer][None, :]
        g_post = norm_post[layer][None, :]
        nm = f"{tag}_l{layer}"
        if kind == 0:
            w1, wq_a, wq_b, wk, wv, g_q, g_kv, w_o = mla[ia]
            ia += 1
            qt, k, vt, sg = _prep_call(
                _mla_prep_kernel, nm + "_mla_prep", x,
                (g_pre, w1, g_q[None, :], wq_a, wq_b, g_kv[None, :], wk, wv), _mla_tables(seq),
                MLA_HEADS, MLA_HEADS, (), MLA_V, MLA_WIDTH)
            o = _attention(qt, k, vt, heads_per_step=2, dv=MLA_V, name=nm + "_mla_attn")
        elif kind == 1:
            w, g_q, g_k, w_o = gqa[ib]
            ib += 1
            qt, k, vt, sg = _prep_call(
                _gqa_prep_kernel, nm + "_gqa_prep", x, (g_pre, w, g_q[None, :], g_k[None, :]),
                _gqa_tables(seq), GQA_HEADS, GQA_KV_HEADS, (), GQA_HEAD_DIM, GQA_WIDTH)
            o = _attention(qt, k, vt, heads_per_step=2, dv=GQA_HEAD_DIM, name=nm + "_gqa_attn")
        else:
            w, lam_vecs, g_sub, w_o = dif[ic]
            ic += 1
            lam_init = 0.8 - 0.6 * math.exp(-0.3 * layer)
            qt, k, vt, sg = _prep_call(
                _diff_prep_kernel, nm + "_dif_prep", x, (g_pre, w), (),
                DIFF_HEADS, DIFF_HEADS, (2,), 2 * DIFF_HEAD_DIM, DIFF_WIDTH)
            o = _diff_attention(qt, k, vt, bias_tiles, lam_vecs, g_sub[:, None],
                                lam_init=lam_init, name=nm + "_dif_attn")
        x = _out_call(x, o, sg, w_o, g_post, nm + "_out")
    return x


def kernel(x_prompt, x_sample, norm_pre, norm_post, rel_bias, mla_w_in, mla_g_q, mla_w_uq, mla_g_kv, mla_w_ukv, mla_w_o, gqa_w_in, gqa_g_q, gqa_g_k, gqa_w_o, dif_w_in, dif_lam_q1, dif_lam_k1, dif_lam_q2, dif_lam_k2, dif_g_sub, dif_w_o):
    mla = []
    for i in range(mla_w_in.shape[0]):
        mla.append(_mla_weights(mla_w_in[i], mla_w_uq[i], mla_w_ukv[i])
                   + (mla_g_q[i], mla_g_kv[i], mla_w_o[i].astype(BF16)))
    gqa = [(gqa_w_in[i].astype(BF16), gqa_g_q[i], gqa_g_k[i], gqa_w_o[i].astype(BF16))
           for i in range(gqa_w_in.shape[0])]
    dif = [(dif_w_in[i].astype(BF16),
            jnp.stack([dif_lam_q1[i], dif_lam_k1[i], dif_lam_q2[i], dif_lam_k2[i]]),
            dif_g_sub[i], dif_w_o[i].astype(BF16))
           for i in range(dif_w_in.shape[0])]
    bias_tiles = _bias_tiles(rel_bias)
    y_prompt = _trunk(x_prompt, "p", norm_pre, norm_post, bias_tiles, mla, gqa, dif)
    y_sample = _trunk(x_sample, "s", norm_pre, norm_post, bias_tiles, mla, gqa, dif)
    return (y_prompt, y_sample)
```

```python
import functools
import math

import jax
import jax.numpy as jnp
from jax import lax
from jax.experimental import pallas as pl
from jax.experimental.pallas import tpu as pltpu

D_MODEL = 1024
DEPTH = 4
N_MIXERS = 3
EPS = 1e-6
ROPE_THETA = 10000.0
GRID_W = 64

MLA_HEADS = 16
MLA_Q_LORA = 256
MLA_KV_LORA = 128
MLA_NOPE = 64
MLA_ROPE = 32
MLA_V = 64
MLA_WIDTH = MLA_HEADS * MLA_V

GQA_HEADS = 8
GQA_KV_HEADS = 2
GQA_HEAD_DIM = 128
GQA_REP = GQA_HEADS // GQA_KV_HEADS
GQA_WIDTH = GQA_HEADS * GQA_HEAD_DIM
GQA_KV_WIDTH = GQA_KV_HEADS * GQA_HEAD_DIM

DIFF_HEADS = 8
DIFF_HEAD_DIM = 64
DIFF_WIDTH = DIFF_HEADS * 2 * DIFF_HEAD_DIM

REL_BUCKETS = 32
REL_MAX_DIST = 128

LANES = 128
BF16_SUBLANES = 16
TOKEN_TILE = 512
KEY_TILE = 1024
VMEM_LIMIT = 56 * 1024 * 1024
LOG2E = math.log2(math.e)

F32 = jnp.float32
BF16 = jnp.bfloat16


def _cparams(semantics):
    return pltpu.CompilerParams(dimension_semantics=semantics, vmem_limit_bytes=VMEM_LIMIT)


def _rms(x, g):
    return x * lax.rsqrt(jnp.mean(x * x, axis=-1, keepdims=True) + EPS) * g


def _dot(a, b):
    return jnp.dot(a, b, preferred_element_type=F32)


def _ones_row_block(width):
    row = lax.broadcasted_iota(jnp.int32, (BF16_SUBLANES, width), 0)
    return jnp.where(row == 0, 1.0, 0.0).astype(BF16)


def _silu(x):
    return x * (1.0 / (1.0 + jnp.exp(-x)))


def _mla_prep_kernel(x_ref, gpre_ref, w1_ref, gq_ref, wqa_ref, wqb_ref, gkv_ref, wkk_ref, wkv_ref,
                     cos_ref, sin_ref, qt_ref, k_ref, vt_ref, sg_ref):
    tm = x_ref.shape[1]
    h = _rms(x_ref[0], gpre_ref[...]).astype(BF16)
    proj = _dot(h, w1_ref[...])
    o = 0
    q_lat = proj[:, o:o + MLA_Q_LORA]; o += MLA_Q_LORA
    kv_lat = proj[:, o:o + MLA_KV_LORA]; o += MLA_KV_LORA
    kr_a = proj[:, o:o + LANES]; o += LANES
    kr_b = proj[:, o:o + LANES]; o += LANES
    gate = proj[:, o:o + MLA_WIDTH]
    sg_ref[0] = _silu(gate)

    cos = cos_ref[...]
    sin = sin_ref[...]
    scale = LOG2E / math.sqrt(MLA_NOPE + MLA_ROPE)
    qn = _rms(q_lat, gq_ref[...]).astype(BF16)
    qa = _dot(qn, wqa_ref[...])
    qb = _dot(qn, wqb_ref[...])
    kvn = _rms(kv_lat, gkv_ref[...]).astype(BF16)
    kk = _dot(kvn, wkk_ref[...])
    vv = _dot(kvn, wkv_ref[...])
    k_rot = kr_a * cos + kr_b * sin
    ones_blk = _ones_row_block(tm)
    for hd in range(MLA_HEADS):
        sl = slice(hd * LANES, (hd + 1) * LANES)
        qh = (qa[:, sl] * cos + qb[:, sl] * sin) * scale
        qt_ref[0, hd] = qh.T.astype(BF16)
        k_ref[0, hd] = (kk[:, sl] + k_rot).astype(BF16)
        vt_ref[0, hd, MLA_V:MLA_V + BF16_SUBLANES, :] = ones_blk
    for pr in range(MLA_HEADS // 2):
        vt = vv[:, pr * LANES:(pr + 1) * LANES].T.astype(BF16)
        vt_ref[0, 2 * pr, 0:MLA_V, :] = vt[0:MLA_V]
        vt_ref[0, 2 * pr + 1, 0:MLA_V, :] = vt[MLA_V:2 * MLA_V]


def _gqa_prep_kernel(x_ref, gpre_ref, w_ref, gq_ref, gk_ref, cos_ref, sin_ref,
                     qt_ref, k_ref, vt_ref, sg_ref):
    tm = x_ref.shape[1]
    h = _rms(x_ref[0], gpre_ref[...]).astype(BF16)
    proj = _dot(h, w_ref[...])
    cos = cos_ref[...]
    sin = sin_ref[...]
    scale = LOG2E / math.sqrt(GQA_HEAD_DIM)

    def rot(xh, g):
        xn = _rms(xh, g)
        return xn * cos + pltpu.roll(xn, GQA_HEAD_DIM // 2, axis=1) * sin

    for hd in range(GQA_HEADS):
        qh = rot(proj[:, hd * LANES:(hd + 1) * LANES], gq_ref[...]) * scale
        qt_ref[0, hd] = qh.T.astype(BF16)
    ones_blk = _ones_row_block(tm)
    for g in range(GQA_KV_HEADS):
        ko = GQA_WIDTH + g * LANES
        vo = GQA_WIDTH + GQA_KV_WIDTH + g * LANES
        k_ref[0, g] = rot(proj[:, ko:ko + LANES], gk_ref[...]).astype(BF16)
        vt_ref[0, g, 0:GQA_HEAD_DIM, :] = proj[:, vo:vo + LANES].T.astype(BF16)
        vt_ref[0, g, GQA_HEAD_DIM:GQA_HEAD_DIM + BF16_SUBLANES, :] = ones_blk
    go = GQA_WIDTH + 2 * GQA_KV_WIDTH
    sg_ref[0] = _silu(proj[:, go:go + GQA_WIDTH])


def _diff_prep_kernel(x_ref, gpre_ref, w_ref, qt_ref, k_ref, vt_ref, sg_ref):
    tm = x_ref.shape[1]
    h = _rms(x_ref[0], gpre_ref[...]).astype(BF16)
    proj = _dot(h, w_ref[...])
    scale = LOG2E / math.sqrt(DIFF_HEAD_DIM)
    row = lax.broadcasted_iota(jnp.int32, (LANES, tm), 0)
    ones_blk = _ones_row_block(tm)
    for hd in range(DIFF_HEADS):
        sl = slice(hd * LANES, (hd + 1) * LANES)
        qt = (proj[:, sl] * scale).T
        qt_ref[0, hd, 0] = jnp.where(row < DIFF_HEAD_DIM, qt, 0.0).astype(BF16)
        qt_ref[0, hd, 1] = jnp.where(row >= DIFF_HEAD_DIM, qt, 0.0).astype(BF16)
        k_ref[0, hd] = proj[:, DIFF_WIDTH + hd * LANES:DIFF_WIDTH + (hd + 1) * LANES].astype(BF16)
        vo = 2 * DIFF_WIDTH + hd * LANES
        vt_ref[0, hd, 0:LANES, :] = proj[:, vo:vo + LANES].T.astype(BF16)
        vt_ref[0, hd, LANES:LANES + BF16_SUBLANES, :] = ones_blk
    sg_ref[0] = _silu(proj[:, 3 * DIFF_WIDTH:4 * DIFF_WIDTH])


def _full(shape):
    return pl.BlockSpec(shape, lambda b, t: (0,) * len(shape))


def _prep_call(kernel, name, x, consts, tables, n_heads, n_kv, qt_inner, dv, sg_width):
    bsz, seq, _ = x.shape
    tm = TOKEN_TILE
    in_specs = [pl.BlockSpec((1, tm, D_MODEL), lambda b, t: (b, t, 0))]
    in_specs += [_full(c.shape) for c in consts]
    in_specs += [pl.BlockSpec((tm, LANES), lambda b, t: (t, 0)) for _ in tables]
    qt_shape = (bsz, n_heads) + qt_inner + (LANES, seq)
    qt_block = (1, n_heads) + qt_inner + (LANES, tm)
    nq = len(qt_shape)
    out_shape = (
        jax.ShapeDtypeStruct(qt_shape, BF16),
        jax.ShapeDtypeStruct((bsz, n_kv, seq, LANES), BF16),
        jax.ShapeDtypeStruct((bsz, n_kv, dv + BF16_SUBLANES, seq), BF16),
        jax.ShapeDtypeStruct((bsz, seq, sg_width), F32),
    )
    out_specs = (
        pl.BlockSpec(qt_block, lambda b, t: (b,) + (0,) * (nq - 2) + (t,)),
        pl.BlockSpec((1, n_kv, tm, LANES), lambda b, t: (b, 0, t, 0)),
        pl.BlockSpec((1, n_kv, dv + BF16_SUBLANES, tm), lambda b, t: (b, 0, 0, t)),
        pl.BlockSpec((1, tm, sg_width), lambda b, t: (b, t, 0)),
    )
    return pl.pallas_call(
        kernel, grid=(bsz, seq // tm), in_specs=in_specs, out_specs=out_specs, out_shape=out_shape,
        compiler_params=_cparams(("parallel", "parallel")), name=name,
    )(x, *consts, *tables)


def _flash_loop(scores, values, n_streams, n_tiles, unroll, sa_ref, sb_ref, tmax_ref, m_ref, acc_ref):
    assert unroll % 2 == 0 and n_tiles % unroll == 0
    m_ref[...] = jnp.full(m_ref.shape, -jnp.inf, F32)
    acc_ref[...] = jnp.zeros(acc_ref.shape, F32)
    bufs = (sa_ref, sb_ref)

    def fill(j, slot):
        for st in range(n_streams):
            s = scores(st, j)
            bufs[slot][st] = s
            tmax_ref[slot, st] = jnp.max(s, axis=0, keepdims=True)

    def consume(j, slot):
        for st in range(n_streams):
            m_prev = m_ref[st]
            m_new = jnp.maximum(m_prev, tmax_ref[slot, st])
            p = jnp.exp2(bufs[slot][st] - m_new).astype(BF16)
            alpha = jnp.exp2(m_prev - m_new)
            acc_ref[st] = acc_ref[st] * alpha + _dot(values(st, j), p)
            m_ref[st] = m_new

    def trip(i, last):
        for u in range(unroll):
            if not (last and u == unroll - 1):
                fill(unroll * i + u + 1, (u + 1) % 2)
            consume(unroll * i + u, u % 2)

    fill(0, 0)

    def body(i, carry):
        trip(i, False)
        return carry

    lax.fori_loop(0, n_tiles // unroll - 1, body, 0)
    trip(n_tiles // unroll - 1, True)


def _tile_start(j, tk):
    return j * tk if isinstance(j, int) else pl.multiple_of(j * tk, tk)


def _attn_kernel(qt_ref, k_ref, vt_ref, o_ref, sa_ref, sb_ref, tmax_ref, m_ref, acc_ref, *, heads, dv, tk):
    seq = k_ref.shape[2]
    shared_kv = k_ref.shape[1] == 1

    def scores(hd, j):
        kv = 0 if shared_kv else hd
        return _dot(k_ref[0, kv, pl.ds(_tile_start(j, tk), tk), :], qt_ref[0, hd])

    def values(hd, j):
        kv = 0 if shared_kv else hd
        return vt_ref[0, kv, :, pl.ds(_tile_start(j, tk), tk)]

    _flash_loop(scores, values, heads, seq // tk, 2, sa_ref, sb_ref, tmax_ref, m_ref, acc_ref)
    outs = []
    for hd in range(heads):
        acc = acc_ref[hd]
        outs.append(acc[0:dv] * (1.0 / acc[dv:dv + 1]))
    o_t = outs[0] if heads == 1 else jnp.concatenate(outs, axis=0)
    for c in range(heads * dv // LANES):
        o_ref[0, c] = o_t[c * LANES:(c + 1) * LANES].T


def _attention(qt, k, vt, *, heads_per_step, dv, name):
    bsz, n_heads, _, seq = qt.shape
    n_kv = k.shape[1]
    tq = TOKEN_TILE
    tk = KEY_TILE
    hps = heads_per_step
    rep = n_heads // n_kv
    kv_per_step = 1 if rep > 1 else hps
    assert (rep == 1) or (rep % hps == 0)
    kv_index = (lambda h: (h * hps) // rep) if rep > 1 else (lambda h: h)
    cols = hps * dv // LANES
    dvp = dv + BF16_SUBLANES
    kernel = functools.partial(_attn_kernel, heads=hps, dv=dv, tk=tk)
    return pl.pallas_call(
        kernel,
        grid=(bsz, n_heads // hps, seq // tq),
        in_specs=[
            pl.BlockSpec((1, hps, LANES, tq), lambda b, h, q: (b, h, 0, q)),
            pl.BlockSpec((1, kv_per_step, seq, LANES), lambda b, h, q: (b, kv_index(h), 0, 0)),
            pl.BlockSpec((1, kv_per_step, dvp, seq), lambda b, h, q: (b, kv_index(h), 0, 0)),
        ],
        out_specs=pl.BlockSpec((1, cols, tq, LANES), lambda b, h, q: (b, h, q, 0)),
        out_shape=jax.ShapeDtypeStruct((bsz, n_heads * dv // LANES, seq, LANES), F32),
        scratch_shapes=[pltpu.VMEM((hps, tk, tq), F32), pltpu.VMEM((hps, tk, tq), F32),
                        pltpu.VMEM((2, hps, 1, tq), F32),
                        pltpu.VMEM((hps, 1, tq), F32), pltpu.VMEM((hps, dvp, tq), F32)],
        compiler_params=_cparams(("parallel", "parallel", "arbitrary")), name=name,
    )(qt, k, vt)


def _diff_attn_kernel(lam_ref, gsub_ref, qt_ref, k_ref, vt_ref, bias_ref, o_ref,
                      sa_ref, sb_ref, tmax_ref, m_ref, acc_ref, *, tk, lam_init):
    seq = k_ref.shape[2]
    qi = pl.program_id(2)

    def scores(c, j):
        bias = bias_ref[0, jnp.clip(j - qi, -2, 2) + 2]
        return _dot(k_ref[0, 0, pl.ds(_tile_start(j, tk), tk), :], qt_ref[0, 0, c]) + bias

    def values(c, j):
        return vt_ref[0, 0, :, pl.ds(_tile_start(j, tk), tk)]

    _flash_loop(scores, values, 2, seq // tk, 4, sa_ref, sb_ref, tmax_ref, m_ref, acc_ref)
    lq1, lk1, lq2, lk2 = (lam_ref[i:i + 1, :] for i in range(4))
    lam = (jnp.exp(jnp.sum(lq1 * lk1, axis=1, keepdims=True))
           - jnp.exp(jnp.sum(lq2 * lk2, axis=1, keepdims=True)) + lam_init)
    a1 = acc_ref[0]
    a2 = acc_ref[1]
    o_t = a1[0:LANES] * (1.0 / a1[LANES:LANES + 1]) - lam * (a2[0:LANES] * (1.0 / a2[LANES:LANES + 1]))
    ms = jnp.mean(o_t * o_t, axis=0, keepdims=True)
    o_t = o_t * lax.rsqrt(ms + EPS) * gsub_ref[...] * (1.0 - lam_init)
    o_ref[0, 0] = o_t.T


def _diff_attention(qt, k, vt, bias, lam_vecs, g_sub_col, *, lam_init, name):
    bsz, n_heads, _, _, seq = qt.shape
    tq = tk = TOKEN_TILE
    dvp = LANES + BF16_SUBLANES
    kernel = functools.partial(_diff_attn_kernel, tk=tk, lam_init=lam_init)
    return pl.pallas_call(
        kernel,
        grid=(bsz, n_heads, seq // tq),
        in_specs=[
            pl.BlockSpec((4, DIFF_HEAD_DIM), lambda b, h, q: (0, 0)),
            pl.BlockSpec((LANES, 1), lambda b, h, q: (0, 0)),
            pl.BlockSpec((1, 1, 2, LANES, tq), lambda b, h, q: (b, h, 0, 0, q)),
            pl.BlockSpec((1, 1, seq, LANES), lambda b, h, q: (b, h, 0, 0)),
            pl.BlockSpec((1, 1, dvp, seq), lambda b, h, q: (b, h, 0, 0)),
            pl.BlockSpec((1, 5, tk, tq), lambda b, h, q: (h, 0, 0, 0)),
        ],
        out_specs=pl.BlockSpec((1, 1, tq, LANES), lambda b, h, q: (b, h, q, 0)),
        out_shape=jax.ShapeDtypeStruct((bsz, n_heads, seq, LANES), F32),
        scratch_shapes=[pltpu.VMEM((2, tk, tq), F32), pltpu.VMEM((2, tk, tq), F32),
                        pltpu.VMEM((2, 2, 1, tq), F32),
                        pltpu.VMEM((2, 1, tq), F32), pltpu.VMEM((2, dvp, tq), F32)],
        compiler_params=_cparams(("parallel", "parallel", "arbitrary")), name=name,
    )(lam_vecs, g_sub_col, qt, k, vt, bias)


def _bias_kernel(rel_ref, bucket_ref, o_ref):
    hd = pl.program_id(0)
    n_tiles, t, _ = bucket_ref.shape
    chunk = 64
    per_tile = t // chunk

    def body(c, carry):
        d = c // per_tile
        rows = pl.ds(pl.multiple_of((c % per_tile) * chunk, chunk), chunk)
        idx = bucket_ref[d, rows, :]
        acc = jnp.zeros(idx.shape, F32)
        for b in range(REL_BUCKETS):
            acc = jnp.where(idx == b, rel_ref[b, hd] * LOG2E, acc)
        o_ref[0, d, rows, :] = acc
        return carry

    lax.fori_loop(0, n_tiles * per_tile, body, 0)


def _t5_bucket(rel):
    half = REL_BUCKETS // 2
    max_exact = half // 2
    base = (rel > 0).astype(jnp.int32) * half
    n = jnp.abs(rel)
    nf = jnp.maximum(n, 1).astype(F32)
    large = max_exact + (jnp.log(nf / max_exact) / math.log(REL_MAX_DIST / max_exact)
                         * (half - max_exact)).astype(jnp.int32)
    large = jnp.minimum(large, half - 1)
    return base + jnp.where(n < max_exact, n, large)


def _bias_tiles(rel_bias):
    t = TOKEN_TILE
    assert t >= REL_MAX_DIST
    kk = jnp.arange(t)[:, None]
    qq = jnp.arange(t)[None, :]
    buckets = jnp.stack([_t5_bucket(d * t + kk - qq) for d in range(-2, 3)])
    return pl.pallas_call(
        _bias_kernel,
        grid=(DIFF_HEADS,),
        in_specs=[pl.BlockSpec(memory_space=pltpu.SMEM),
                  pl.BlockSpec((5, t, t), lambda h: (0, 0, 0))],
        out_specs=pl.BlockSpec((1, 5, t, t), lambda h: (h, 0, 0, 0)),
        out_shape=jax.ShapeDtypeStruct((DIFF_HEADS, 5, t, t), F32),
        compiler_params=_cparams(("parallel",)), name="t5_bias_tiles",
    )(rel_bias, buckets)


def _out_kernel(x_ref, o_ref, sg_ref, w_ref, g_ref, y_ref):
    cols = o_ref.shape[1]
    gated = [(o_ref[0, c] * sg_ref[0, :, c * LANES:(c + 1) * LANES]).astype(BF16) for c in range(cols)]
    m = _dot(jnp.concatenate(gated, axis=1), w_ref[...])
    y_ref[0] = x_ref[0] + _rms(m, g_ref[...])


def _out_call(x, o, sg, w_o, g_post, name):
    bsz, seq, _ = x.shape
    tm = TOKEN_TILE
    cols = o.shape[1]
    return pl.pallas_call(
        _out_kernel,
        grid=(bsz, seq // tm),
        in_specs=[
            pl.BlockSpec((1, tm, D_MODEL), lambda b, t: (b, t, 0)),
            pl.BlockSpec((1, cols, tm, LANES), lambda b, t: (b, 0, t, 0)),
            pl.BlockSpec((1, tm, cols * LANES), lambda b, t: (b, t, 0)),
            _full(w_o.shape),
            _full(g_post.shape),
        ],
        out_specs=pl.BlockSpec((1, tm, D_MODEL), lambda b, t: (b, t, 0)),
        out_shape=jax.ShapeDtypeStruct(x.shape, F32),
        compiler_params=_cparams(("parallel", "parallel")), name=name,
    )(x, o, sg, w_o, g_post)


def _rope_angles(pos, dim):
    inv = ROPE_THETA ** (-(jnp.arange(0, dim, 2, dtype=F32) / dim))
    return pos.astype(F32)[:, None] * inv[None, :]


def _mla_tables(seq):
    ang = _rope_angles(jnp.arange(seq), MLA_ROPE)
    cos, sin = jnp.cos(ang), jnp.sin(ang)
    pad = LANES - MLA_NOPE - MLA_ROPE
    cos_t = jnp.concatenate([jnp.ones((seq, MLA_NOPE), F32), cos, cos, jnp.zeros((seq, pad), F32)], axis=1)
    sin_t = jnp.concatenate([jnp.zeros((seq, MLA_NOPE), F32), -sin, sin, jnp.zeros((seq, pad), F32)], axis=1)
    return cos_t, sin_t


def _gqa_tables(seq):
    rows = seq // GRID_W
    row = jnp.repeat(jnp.arange(rows), GRID_W)
    col = jnp.tile(jnp.arange(GRID_W), rows)
    ang = jnp.concatenate([_rope_angles(row, GQA_HEAD_DIM // 2), _rope_angles(col, GQA_HEAD_DIM // 2)], axis=-1)
    cos, sin = jnp.cos(ang), jnp.sin(ang)
    return jnp.concatenate([cos, cos], axis=1), jnp.concatenate([-sin, sin], axis=1)


def _mla_weights(w_in, w_uq, w_ukv):
    half = MLA_ROPE // 2
    pad = LANES - MLA_NOPE - MLA_ROPE
    o_kv = MLA_Q_LORA
    o_kr = o_kv + MLA_KV_LORA
    o_gate = o_kr + MLA_ROPE
    w_kr = w_in[:, o_kr:o_gate]
    z = lambda n, rows: jnp.zeros((rows, n), F32)
    kr_a = jnp.concatenate([z(MLA_NOPE, D_MODEL), w_kr, z(pad, D_MODEL)], axis=1)
    kr_b = jnp.concatenate([z(MLA_NOPE, D_MODEL), w_kr[:, half:], w_kr[:, :half], z(pad, D_MODEL)], axis=1)
    w1 = jnp.concatenate([w_in[:, :o_kr], kr_a, kr_b, w_in[:, o_gate:]], axis=1)
    uq = w_uq.reshape(MLA_Q_LORA, MLA_HEADS, MLA_NOPE + MLA_ROPE)
    nope, r1, r2 = uq[..., :MLA_NOPE], uq[..., MLA_NOPE:MLA_NOPE + half], uq[..., MLA_NOPE + half:]
    zq = jnp.zeros((MLA_Q_LORA, MLA_HEADS, pad), F32)
    wq_a = jnp.concatenate([nope, r1, r2, zq], axis=-1).reshape(MLA_Q_LORA, MLA_HEADS * LANES)
    wq_b = jnp.concatenate([jnp.zeros_like(nope), r2, r1, zq], axis=-1).reshape(MLA_Q_LORA, MLA_HEADS * LANES)
    ukv = w_ukv.reshape(MLA_KV_LORA, MLA_HEADS, MLA_NOPE + MLA_V)
    wk = jnp.concatenate([ukv[..., :MLA_NOPE], jnp.zeros((MLA_KV_LORA, MLA_HEADS, LANES - MLA_NOPE), F32)],
                         axis=-1).reshape(MLA_KV_LORA, MLA_HEADS * LANES)
    wv = ukv[..., MLA_NOPE:].reshape(MLA_KV_LORA, MLA_WIDTH)
    return tuple(w.astype(BF16) for w in (w1, wq_a, wq_b, wk, wv))


def _trunk(x, tag, norm_pre, norm_post, bias_tiles, mla, gqa, dif):
    bsz, seq, _ = x.shape
    ia = ib = ic = 0
    for layer in range(DEPTH):
        kind = layer % N_MIXERS
        g_pre = norm_pre[layer][None, :]
        g_post = norm_post[layer][None, :]
        nm = f"{tag}_l{layer}"
        if kind == 0:
            w1, wq_a, wq_b, wk, wv, g_q, g_kv, w_o = mla[ia]
            ia += 1
            qt, k, vt, sg = _prep_call(
                _mla_prep_kernel, nm + "_mla_prep", x,
                (g_pre, w1, g_q[None, :], wq_a, wq_b, g_kv[None, :], wk, wv), _mla_tables(seq),
                MLA_HEADS, MLA_HEADS, (), MLA_V, MLA_WIDTH)
            o = _attention(qt, k, vt, heads_per_step=2, dv=MLA_V, name=nm + "_mla_attn")
        elif kind == 1:
            w, g_q, g_k, w_o = gqa[ib]
            ib += 1
            qt, k, vt, sg = _prep_call(
                _gqa_prep_kernel, nm + "_gqa_prep", x, (g_pre, w, g_q[None, :], g_k[None, :]),
                _gqa_tables(seq), GQA_HEADS, GQA_KV_HEADS, (), GQA_HEAD_DIM, GQA_WIDTH)
            o = _attention(qt, k, vt, heads_per_step=2, dv=GQA_HEAD_DIM, name=nm + "_gqa_attn")
        else:
            w, lam_vecs, g_sub, w_o = dif[ic]
            ic += 1
            lam_init = 0.8 - 0.6 * math.exp(-0.3 * layer)
            qt, k, vt, sg = _prep_call(
                _diff_prep_kernel, nm + "_dif_prep", x, (g_pre, w), (),
                DIFF_HEADS, DIFF_HEADS, (2,), 2 * DIFF_HEAD_DIM, DIFF_WIDTH)
            o = _diff_attention(qt, k, vt, bias_tiles, lam_vecs, g_sub[:, None],
                                lam_init=lam_init, name=nm + "_dif_attn")
        x = _out_call(x, o, sg, w_o, g_post, nm + "_out")
    return x


def kernel(x_prompt, x_sample, norm_pre, norm_post, rel_bias, mla_w_in, mla_g_q, mla_w_uq, mla_g_kv, mla_w_ukv, mla_w_o, gqa_w_in, gqa_g_q, gqa_g_k, gqa_w_o, dif_w_in, dif_lam_q1, dif_lam_k1, dif_lam_q2, dif_lam_k2, dif_g_sub, dif_w_o):
    mla = []
    for i in range(mla_w_in.shape[0]):
        mla.append(_mla_weights(mla_w_in[i], mla_w_uq[i], mla_w_ukv[i])
                   + (mla_g_q[i], mla_g_kv[i], mla_w_o[i].astype(BF16)))
    gqa = [(gqa_w_in[i].astype(BF16), gqa_g_q[i], gqa_g_k[i], gqa_w_o[i].astype(BF16))
           for i in range(gqa_w_in.shape[0])]
    dif = [(dif_w_in[i].astype(BF16),
            jnp.stack([dif_lam_q1[i], dif_lam_k1[i], dif_lam_q2[i], dif_lam_k2[i]]),
            dif_g_sub[i], dif_w_o[i].astype(BF16))
           for i in range(dif_w_in.shape[0])]
    bias_tiles = _bias_tiles(rel_bias)
    y_prompt = _trunk(x_prompt, "p", norm_pre, norm_post, bias_tiles, mla, gqa, dif)
    y_sample = _trunk(x_sample, "s", norm_pre, norm_post, bias_tiles, mla, gqa, dif)
    return (y_prompt, y_sample)
```

```python
import functools
import math

import jax
import jax.numpy as jnp
from jax import lax
from jax.experimental import pallas as pl
from jax.experimental.pallas import tpu as pltpu

D_MODEL = 1024
DEPTH = 4
N_MIXERS = 3
EPS = 1e-6
ROPE_THETA = 10000.0
GRID_W = 64

MLA_HEADS = 16
MLA_Q_LORA = 256
MLA_KV_LORA = 128
MLA_NOPE = 64
MLA_ROPE = 32
MLA_V = 64
MLA_WIDTH = MLA_HEADS * MLA_V

GQA_HEADS = 8
GQA_KV_HEADS = 2
GQA_HEAD_DIM = 128
GQA_REP = GQA_HEADS // GQA_KV_HEADS
GQA_WIDTH = GQA_HEADS * GQA_HEAD_DIM
GQA_KV_WIDTH = GQA_KV_HEADS * GQA_HEAD_DIM

DIFF_HEADS = 8
DIFF_HEAD_DIM = 64
DIFF_WIDTH = DIFF_HEADS * 2 * DIFF_HEAD_DIM

REL_BUCKETS = 32
REL_MAX_DIST = 128

LANES = 128
BF16_SUBLANES = 16
TOKEN_TILE = 512
KEY_TILE = 1024
Q_TILES_PER_STEP = 8
KEY_TILES_PER_TRIP = 4
MAX_STATIC_PAIRS = 8
VMEM_LIMIT = 56 * 1024 * 1024
LOG2E = math.log2(math.e)

F32 = jnp.float32
BF16 = jnp.bfloat16


def _cparams(semantics):
    return pltpu.CompilerParams(dimension_semantics=semantics, vmem_limit_bytes=VMEM_LIMIT)


def _rms(x, g):
    return x * lax.rsqrt(jnp.mean(x * x, axis=-1, keepdims=True) + EPS) * g


def _dot(a, b):
    return jnp.dot(a, b, preferred_element_type=F32)


def _ones_row_block(width):
    row = lax.broadcasted_iota(jnp.int32, (BF16_SUBLANES, width), 0)
    return jnp.where(row == 0, 1.0, 0.0).astype(BF16)


def _silu(x):
    return x * (1.0 / (1.0 + jnp.exp(-x)))


def _mla_prep_kernel(x_ref, gpre_ref, w1_ref, gq_ref, wqa_ref, wqb_ref, gkv_ref, wkk_ref, wkv_ref,
                     cos_ref, sin_ref, qt_ref, k_ref, vt_ref, sg_ref):
    tm = x_ref.shape[1]
    h = _rms(x_ref[0], gpre_ref[...]).astype(BF16)
    proj = _dot(h, w1_ref[...])
    o = 0
    q_lat = proj[:, o:o + MLA_Q_LORA]; o += MLA_Q_LORA
    kv_lat = proj[:, o:o + MLA_KV_LORA]; o += MLA_KV_LORA
    kr_a = proj[:, o:o + LANES]; o += LANES
    kr_b = proj[:, o:o + LANES]; o += LANES
    gate = proj[:, o:o + MLA_WIDTH]
    sg_ref[0] = _silu(gate).astype(BF16)

    cos = cos_ref[...]
    sin = sin_ref[...]
    scale = LOG2E / math.sqrt(MLA_NOPE + MLA_ROPE)
    qn = _rms(q_lat, gq_ref[...]).astype(BF16)
    qa = _dot(qn, wqa_ref[...])
    qb = _dot(qn, wqb_ref[...])
    kvn = _rms(kv_lat, gkv_ref[...]).astype(BF16)
    kk = _dot(kvn, wkk_ref[...])
    vv = _dot(kvn, wkv_ref[...])
    k_rot = kr_a * cos + kr_b * sin
    ones_blk = _ones_row_block(tm)
    for hd in range(MLA_HEADS):
        sl = slice(hd * LANES, (hd + 1) * LANES)
        qh = (qa[:, sl] * cos + qb[:, sl] * sin) * scale
        qt_ref[0, hd] = qh.T.astype(BF16)
        k_ref[0, hd] = (kk[:, sl] + k_rot).astype(BF16)
        vt_ref[0, hd, MLA_V:MLA_V + BF16_SUBLANES, :] = ones_blk
    for pr in range(MLA_HEADS // 2):
        vt = vv[:, pr * LANES:(pr + 1) * LANES].T.astype(BF16)
        vt_ref[0, 2 * pr, 0:MLA_V, :] = vt[0:MLA_V]
        vt_ref[0, 2 * pr + 1, 0:MLA_V, :] = vt[MLA_V:2 * MLA_V]


def _gqa_prep_kernel(x_ref, gpre_ref, w_ref, gq_ref, gk_ref, cos_ref, sin_ref,
                     qt_ref, k_ref, vt_ref, sg_ref):
    tm = x_ref.shape[1]
    h = _rms(x_ref[0], gpre_ref[...]).astype(BF16)
    proj = _dot(h, w_ref[...])
    cos = cos_ref[...]
    sin = sin_ref[...]
    scale = LOG2E / math.sqrt(GQA_HEAD_DIM)

    def rot(xh, g):
        xn = _rms(xh, g)
        return xn * cos + pltpu.roll(xn, GQA_HEAD_DIM // 2, axis=1) * sin

    for hd in range(GQA_HEADS):
        qh = rot(proj[:, hd * LANES:(hd + 1) * LANES], gq_ref[...]) * scale
        qt_ref[0, hd] = qh.T.astype(BF16)
    ones_blk = _ones_row_block(tm)
    for g in range(GQA_KV_HEADS):
        ko = GQA_WIDTH + g * LANES
        vo = GQA_WIDTH + GQA_KV_WIDTH + g * LANES
        k_ref[0, g] = rot(proj[:, ko:ko + LANES], gk_ref[...]).astype(BF16)
        vt_ref[0, g, 0:GQA_HEAD_DIM, :] = proj[:, vo:vo + LANES].T.astype(BF16)
        vt_ref[0, g, GQA_HEAD_DIM:GQA_HEAD_DIM + BF16_SUBLANES, :] = ones_blk
    go = GQA_WIDTH + 2 * GQA_KV_WIDTH
    sg_ref[0] = _silu(proj[:, go:go + GQA_WIDTH]).astype(BF16)


def _diff_prep_kernel(x_ref, gpre_ref, w_ref, qt_ref, k_ref, vt_ref, sg_ref):
    tm = x_ref.shape[1]
    h = _rms(x_ref[0], gpre_ref[...]).astype(BF16)
    proj = _dot(h, w_ref[...])
    scale = LOG2E / math.sqrt(DIFF_HEAD_DIM)
    row = lax.broadcasted_iota(jnp.int32, (LANES, tm), 0)
    ones_blk = _ones_row_block(tm)
    for hd in range(DIFF_HEADS):
        sl = slice(hd * LANES, (hd + 1) * LANES)
        qt = (proj[:, sl] * scale).T
        qt_ref[0, hd, 0] = jnp.where(row < DIFF_HEAD_DIM, qt, 0.0).astype(BF16)
        qt_ref[0, hd, 1] = jnp.where(row >= DIFF_HEAD_DIM, qt, 0.0).astype(BF16)
        k_ref[0, hd] = proj[:, DIFF_WIDTH + hd * LANES:DIFF_WIDTH + (hd + 1) * LANES].astype(BF16)
        vo = 2 * DIFF_WIDTH + hd * LANES
        vt_ref[0, hd, 0:LANES, :] = proj[:, vo:vo + LANES].T.astype(BF16)
        vt_ref[0, hd, LANES:LANES + BF16_SUBLANES, :] = ones_blk
    sg_ref[0] = _silu(proj[:, 3 * DIFF_WIDTH:4 * DIFF_WIDTH]).astype(BF16)


def _full(shape):
    return pl.BlockSpec(shape, lambda b, t: (0,) * len(shape))


def _prep_call(kernel, name, x, consts, tables, n_heads, n_kv, qt_inner, dv, sg_width):
    bsz, seq, _ = x.shape
    tm = TOKEN_TILE
    in_specs = [pl.BlockSpec((1, tm, D_MODEL), lambda b, t: (b, t, 0))]
    in_specs += [_full(c.shape) for c in consts]
    in_specs += [pl.BlockSpec((tm, LANES), lambda b, t: (t, 0)) for _ in tables]
    qt_shape = (bsz, n_heads) + qt_inner + (LANES, seq)
    qt_block = (1, n_heads) + qt_inner + (LANES, tm)
    nq = len(qt_shape)
    out_shape = (
        jax.ShapeDtypeStruct(qt_shape, BF16),
        jax.ShapeDtypeStruct((bsz, n_kv, seq, LANES), BF16),
        jax.ShapeDtypeStruct((bsz, n_kv, dv + BF16_SUBLANES, seq), BF16),
        jax.ShapeDtypeStruct((bsz, seq, sg_width), BF16),
    )
    out_specs = (
        pl.BlockSpec(qt_block, lambda b, t: (b,) + (0,) * (nq - 2) + (t,)),
        pl.BlockSpec((1, n_kv, tm, LANES), lambda b, t: (b, 0, t, 0)),
        pl.BlockSpec((1, n_kv, dv + BF16_SUBLANES, tm), lambda b, t: (b, 0, 0, t)),
        pl.BlockSpec((1, tm, sg_width), lambda b, t: (b, t, 0)),
    )
    return pl.pallas_call(
        kernel, grid=(bsz, seq // tm), in_specs=in_specs, out_specs=out_specs, out_shape=out_shape,
        compiler_params=_cparams(("parallel", "parallel")), name=name,
    )(x, *consts, *tables)


def _flash_loop(scores, values, finalize, n_streams, n_q, n_k, sa_ref, sb_ref, tmax_ref, m_ref, acc_ref):
    unroll = min(KEY_TILES_PER_TRIP, n_k)
    static_q = n_q * n_k <= MAX_STATIC_PAIRS
    assert unroll % 2 == 0 and n_k % unroll == 0
    m_ref[...] = jnp.full(m_ref.shape, -jnp.inf, F32)
    acc_ref[...] = jnp.zeros(acc_ref.shape, F32)
    bufs = (sa_ref, sb_ref)
    trips = n_k // unroll

    def fill(qi, j, slot):
        for st in range(n_streams):
            s = scores(st, qi, j)
            bufs[slot][st] = s
            tmax_ref[slot, st] = jnp.max(s, axis=0, keepdims=True)

    def consume(j, slot):
        for st in range(n_streams):
            m_prev = m_ref[st]
            m_new = jnp.maximum(m_prev, tmax_ref[slot, st])
            p = jnp.exp2(bufs[slot][st] - m_new).astype(BF16)
            alpha = jnp.exp2(m_prev - m_new)
            acc_ref[st] = acc_ref[st] * alpha + _dot(values(st, j), p)
            m_ref[st] = m_new

    def trip(qi, i, last):
        for u in range(unroll):
            j = unroll * i + u
            if not (last and u == unroll - 1):
                fill(qi, j + 1, (u + 1) % 2)
            elif not isinstance(qi, int):
                fill(jnp.minimum(qi + 1, n_q - 1), 0, 0)
            elif qi + 1 < n_q:
                fill(qi + 1, 0, 0)
            consume(j, u % 2)

    def query_tile(qi):
        def body(i, carry):
            trip(qi, i, False)
            return carry

        lax.fori_loop(0, trips - 1, body, 0)
        trip(qi, trips - 1, True)
        finalize(qi)
        m_ref[...] = jnp.full(m_ref.shape, -jnp.inf, F32)

    fill(0, 0, 0)
    if static_q:
        for qi in range(n_q):
            query_tile(qi)
    else:
        def q_body(qi, carry):
            query_tile(qi)
            return carry

        lax.fori_loop(0, n_q, q_body, 0)


def _tile_start(j, size):
    return j * size if isinstance(j, int) else pl.multiple_of(j * size, size)


def _attn_kernel(qt_ref, k_ref, vt_ref, o_ref, sa_ref, sb_ref, tmax_ref, m_ref, acc_ref,
                 *, heads, dv, tq, tk):
    shared_kv = k_ref.shape[1] == 1

    def scores(hd, qi, j):
        kv = 0 if shared_kv else hd
        return _dot(k_ref[0, kv, pl.ds(_tile_start(j, tk), tk), :],
                    qt_ref[0, hd, :, pl.ds(_tile_start(qi, tq), tq)])

    def values(hd, j):
        kv = 0 if shared_kv else hd
        return vt_ref[0, kv, :, pl.ds(_tile_start(j, tk), tk)]

    def finalize(qi):
        outs = []
        for hd in range(heads):
            acc = acc_ref[hd]
            outs.append(acc[0:dv] * (1.0 / acc[dv:dv + 1]))
        o_t = outs[0] if heads == 1 else jnp.concatenate(outs, axis=0)
        for c in range(heads * dv // LANES):
            o_ref[0, c, pl.ds(_tile_start(qi, tq), tq), :] = o_t[c * LANES:(c + 1) * LANES].T.astype(BF16)

    _flash_loop(scores, values, finalize, heads, qt_ref.shape[3] // tq, k_ref.shape[2] // tk,
                sa_ref, sb_ref, tmax_ref, m_ref, acc_ref)


def _attention(qt, k, vt, *, heads_per_step, dv, name):
    bsz, n_heads, _, seq = qt.shape
    n_kv = k.shape[1]
    tq = TOKEN_TILE
    tk = KEY_TILE
    tqs = tq * min(Q_TILES_PER_STEP, seq // tq)
    hps = heads_per_step
    rep = n_heads // n_kv
    kv_per_step = 1 if rep > 1 else hps
    assert (rep == 1) or (rep % hps == 0)
    kv_index = (lambda h: (h * hps) // rep) if rep > 1 else (lambda h: h)
    cols = hps * dv // LANES
    dvp = dv + BF16_SUBLANES
    kernel = functools.partial(_attn_kernel, heads=hps, dv=dv, tq=tq, tk=tk)
    return pl.pallas_call(
        kernel,
        grid=(bsz, n_heads // hps, seq // tqs),
        in_specs=[
            pl.BlockSpec((1, hps, LANES, tqs), lambda b, h, q: (b, h, 0, q)),
            pl.BlockSpec((1, kv_per_step, seq, LANES), lambda b, h, q: (b, kv_index(h), 0, 0)),
            pl.BlockSpec((1, kv_per_step, dvp, seq), lambda b, h, q: (b, kv_index(h), 0, 0)),
        ],
        out_specs=pl.BlockSpec((1, cols, tqs, LANES), lambda b, h, q: (b, h, q, 0)),
        out_shape=jax.ShapeDtypeStruct((bsz, n_heads * dv // LANES, seq, LANES), BF16),
        scratch_shapes=[pltpu.VMEM((hps, tk, tq), F32), pltpu.VMEM((hps, tk, tq), F32),
                        pltpu.VMEM((2, hps, 1, tq), F32),
                        pltpu.VMEM((hps, 1, tq), F32), pltpu.VMEM((hps, dvp, tq), F32)],
        compiler_params=_cparams(("parallel", "parallel", "arbitrary")), name=name,
    )(qt, k, vt)


def _diff_attn_kernel(lam_ref, gsub_ref, qt_ref, k_ref, vt_ref, bias_ref, o_ref,
                      sa_ref, sb_ref, tmax_ref, m_ref, acc_ref, *, tq, tk, lam_init):
    n_q = qt_ref.shape[4] // tq
    q_base = pl.program_id(2) * n_q

    def scores(c, qi, j):
        bias = bias_ref[0, jnp.clip(j - (q_base + qi), -2, 2) + 2]
        return _dot(k_ref[0, 0, pl.ds(_tile_start(j, tk), tk), :],
                    qt_ref[0, 0, c, :, pl.ds(_tile_start(qi, tq), tq)]) + bias

    def values(c, j):
        return vt_ref[0, 0, :, pl.ds(_tile_start(j, tk), tk)]

    def finalize(qi):
        lq1, lk1, lq2, lk2 = (lam_ref[i:i + 1, :] for i in range(4))
        lam = (jnp.exp(jnp.sum(lq1 * lk1, axis=1, keepdims=True))
               - jnp.exp(jnp.sum(lq2 * lk2, axis=1, keepdims=True)) + lam_init)
        a1 = acc_ref[0]
        a2 = acc_ref[1]
        o_t = (a1[0:LANES] * (1.0 / a1[LANES:LANES + 1])
               - lam * (a2[0:LANES] * (1.0 / a2[LANES:LANES + 1])))
        ms = jnp.mean(o_t * o_t, axis=0, keepdims=True)
        o_t = o_t * lax.rsqrt(ms + EPS) * gsub_ref[...] * (1.0 - lam_init)
        o_ref[0, 0, pl.ds(_tile_start(qi, tq), tq), :] = o_t.T.astype(BF16)

    _flash_loop(scores, values, finalize, 2, n_q, k_ref.shape[2] // tk,
                sa_ref, sb_ref, tmax_ref, m_ref, acc_ref)


def _diff_attention(qt, k, vt, bias, lam_vecs, g_sub_col, *, lam_init, name):
    bsz, n_heads, _, _, seq = qt.shape
    tq = tk = TOKEN_TILE
    tqs = tq * min(Q_TILES_PER_STEP, seq // tq)
    dvp = LANES + BF16_SUBLANES
    kernel = functools.partial(_diff_attn_kernel, tq=tq, tk=tk, lam_init=lam_init)
    return pl.pallas_call(
        kernel,
        grid=(bsz, n_heads, seq // tqs),
        in_specs=[
            pl.BlockSpec((4, DIFF_HEAD_DIM), lambda b, h, q: (0, 0)),
            pl.BlockSpec((LANES, 1), lambda b, h, q: (0, 0)),
            pl.BlockSpec((1, 1, 2, LANES, tqs), lambda b, h, q: (b, h, 0, 0, q)),
            pl.BlockSpec((1, 1, seq, LANES), lambda b, h, q: (b, h, 0, 0)),
            pl.BlockSpec((1, 1, dvp, seq), lambda b, h, q: (b, h, 0, 0)),
            pl.BlockSpec((1, 5, tk, tq), lambda b, h, q: (h, 0, 0, 0)),
        ],
        out_specs=pl.BlockSpec((1, 1, tqs, LANES), lambda b, h, q: (b, h, q, 0)),
        out_shape=jax.ShapeDtypeStruct((bsz, n_heads, seq, LANES), BF16),
        scratch_shapes=[pltpu.VMEM((2, tk, tq), F32), pltpu.VMEM((2, tk, tq), F32),
                        pltpu.VMEM((2, 2, 1, tq), F32),
                        pltpu.VMEM((2, 1, tq), F32), pltpu.VMEM((2, dvp, tq), F32)],
        compiler_params=_cparams(("parallel", "parallel", "arbitrary")), name=name,
    )(lam_vecs, g_sub_col, qt, k, vt, bias)


def _bias_kernel(rel_ref, bucket_ref, o_ref):
    hd = pl.program_id(0)
    n_tiles, t, _ = bucket_ref.shape
    chunk = 64
    per_tile = t // chunk

    def body(c, carry):
        d = c // per_tile
        rows = pl.ds(pl.multiple_of((c % per_tile) * chunk, chunk), chunk)
        idx = bucket_ref[d, rows, :]
        acc = jnp.zeros(idx.shape, F32)
        for b in range(REL_BUCKETS):
            acc = jnp.where(idx == b, rel_ref[b, hd] * LOG2E, acc)
        o_ref[0, d, rows, :] = acc
        return carry

    lax.fori_loop(0, n_tiles * per_tile, body, 0)


def _t5_bucket(rel):
    half = REL_BUCKETS // 2
    max_exact = half // 2
    base = (rel > 0).astype(jnp.int32) * half
    n = jnp.abs(rel)
    nf = jnp.maximum(n, 1).astype(F32)
    large = max_exact + (jnp.log(nf / max_exact) / math.log(REL_MAX_DIST / max_exact)
                         * (half - max_exact)).astype(jnp.int32)
    large = jnp.minimum(large, half - 1)
    return base + jnp.where(n < max_exact, n, large)


def _bias_tiles(rel_bias):
    t = TOKEN_TILE
    assert t >= REL_MAX_DIST
    kk = jnp.arange(t)[:, None]
    qq = jnp.arange(t)[None, :]
    buckets = jnp.stack([_t5_bucket(d * t + kk - qq) for d in range(-2, 3)])
    return pl.pallas_call(
        _bias_kernel,
        grid=(DIFF_HEADS,),
        in_specs=[pl.BlockSpec(memory_space=pltpu.SMEM),
                  pl.BlockSpec((5, t, t), lambda h: (0, 0, 0))],
        out_specs=pl.BlockSpec((1, 5, t, t), lambda h: (h, 0, 0, 0)),
        out_shape=jax.ShapeDtypeStruct((DIFF_HEADS, 5, t, t), F32),
        compiler_params=_cparams(("parallel",)), name="t5_bias_tiles",
    )(rel_bias, buckets)


def _out_kernel(x_ref, o_ref, sg_ref, w_ref, g_ref, y_ref):
    cols = o_ref.shape[1]
    gated = [(o_ref[0, c].astype(F32) * sg_ref[0, :, c * LANES:(c + 1) * LANES].astype(F32)).astype(BF16)
             for c in range(cols)]
    m = _dot(jnp.concatenate(gated, axis=1), w_ref[...])
    y_ref[0] = x_ref[0] + _rms(m, g_ref[...])


def _out_call(x, o, sg, w_o, g_post, name):
    bsz, seq, _ = x.shape
    tm = TOKEN_TILE
    cols = o.shape[1]
    return pl.pallas_call(
        _out_kernel,
        grid=(bsz, seq // tm),
        in_specs=[
            pl.BlockSpec((1, tm, D_MODEL), lambda b, t: (b, t, 0)),
            pl.BlockSpec((1, cols, tm, LANES), lambda b, t: (b, 0, t, 0)),
            pl.BlockSpec((1, tm, cols * LANES), lambda b, t: (b, t, 0)),
            _full(w_o.shape),
            _full(g_post.shape),
        ],
        out_specs=pl.BlockSpec((1, tm, D_MODEL), lambda b, t: (b, t, 0)),
        out_shape=jax.ShapeDtypeStruct(x.shape, F32),
        compiler_params=_cparams(("parallel", "parallel")), name=name,
    )(x, o, sg, w_o, g_post)


def _rope_angles(pos, dim):
    inv = ROPE_THETA ** (-(jnp.arange(0, dim, 2, dtype=F32) / dim))
    return pos.astype(F32)[:, None] * inv[None, :]


def _mla_tables(seq):
    ang = _rope_angles(jnp.arange(seq), MLA_ROPE)
    cos, sin = jnp.cos(ang), jnp.sin(ang)
    pad = LANES - MLA_NOPE - MLA_ROPE
    cos_t = jnp.concatenate([jnp.ones((seq, MLA_NOPE), F32), cos, cos, jnp.zeros((seq, pad), F32)], axis=1)
    sin_t = jnp.concatenate([jnp.zeros((seq, MLA_NOPE), F32), -sin, sin, jnp.zeros((seq, pad), F32)], axis=1)
    return cos_t, sin_t


def _gqa_tables(seq):
    rows = seq // GRID_W
    row = jnp.repeat(jnp.arange(rows), GRID_W)
    col = jnp.tile(jnp.arange(GRID_W), rows)
    ang = jnp.concatenate([_rope_angles(row, GQA_HEAD_DIM // 2), _rope_angles(col, GQA_HEAD_DIM // 2)], axis=-1)
    cos, sin = jnp.cos(ang), jnp.sin(ang)
    return jnp.concatenate([cos, cos], axis=1), jnp.concatenate([-sin, sin], axis=1)


def _mla_weights(w_in, w_uq, w_ukv):
    half = MLA_ROPE // 2
    pad = LANES - MLA_NOPE - MLA_ROPE
    o_kv = MLA_Q_LORA
    o_kr = o_kv + MLA_KV_LORA
    o_gate = o_kr + MLA_ROPE
    w_kr = w_in[:, o_kr:o_gate]
    z = lambda n, rows: jnp.zeros((rows, n), F32)
    kr_a = jnp.concatenate([z(MLA_NOPE, D_MODEL), w_kr, z(pad, D_MODEL)], axis=1)
    kr_b = jnp.concatenate([z(MLA_NOPE, D_MODEL), w_kr[:, half:], w_kr[:, :half], z(pad, D_MODEL)], axis=1)
    w1 = jnp.concatenate([w_in[:, :o_kr], kr_a, kr_b, w_in[:, o_gate:]], axis=1)
    uq = w_uq.reshape(MLA_Q_LORA, MLA_HEADS, MLA_NOPE + MLA_ROPE)
    nope, r1, r2 = uq[..., :MLA_NOPE], uq[..., MLA_NOPE:MLA_NOPE + half], uq[..., MLA_NOPE + half:]
    zq = jnp.zeros((MLA_Q_LORA, MLA_HEADS, pad), F32)
    wq_a = jnp.concatenate([nope, r1, r2, zq], axis=-1).reshape(MLA_Q_LORA, MLA_HEADS * LANES)
    wq_b = jnp.concatenate([jnp.zeros_like(nope), r2, r1, zq], axis=-1).reshape(MLA_Q_LORA, MLA_HEADS * LANES)
    ukv = w_ukv.reshape(MLA_KV_LORA, MLA_HEADS, MLA_NOPE + MLA_V)
    wk = jnp.concatenate([ukv[..., :MLA_NOPE], jnp.zeros((MLA_KV_LORA, MLA_HEADS, LANES - MLA_NOPE), F32)],
                         axis=-1).reshape(MLA_KV_LORA, MLA_HEADS * LANES)
    wv = ukv[..., MLA_NOPE:].reshape(MLA_KV_LORA, MLA_WIDTH)
    return tuple(w.astype(BF16) for w in (w1, wq_a, wq_b, wk, wv))


def _trunk(x, tag, norm_pre, norm_post, bias_tiles, mla, gqa, dif):
    bsz, seq, _ = x.shape
    ia = ib = ic = 0
    for layer in range(DEPTH):
        kind = layer % N_MIXERS
        g_pre = norm_pre[layer][None, :]
        g_post = norm_post[layer][None, :]
        nm = f"{tag}_l{layer}"
        if kind == 0:
            w1, wq_a, wq_b, wk, wv, g_q, g_kv, w_o = mla[ia]
            ia += 1
            qt, k, vt, sg = _prep_call(
                _mla_prep_kernel, nm + "_mla_prep", x,
                (g_pre, w1, g_q[None, :], wq_a, wq_b, g_kv[None, :], wk, wv), _mla_tables(seq),
                MLA_HEADS, MLA_HEADS, (), MLA_V, MLA_WIDTH)
            o = _attention(qt, k, vt, heads_per_step=2, dv=MLA_V, name=nm + "_mla_attn")
        elif kind == 1:
            w, g_q, g_k, w_o = gqa[ib]
            ib += 1
            qt, k, vt, sg = _prep_call(
                _gqa_prep_kernel, nm + "_gqa_prep", x, (g_pre, w, g_q[None, :], g_k[None, :]),
                _gqa_tables(seq), GQA_HEADS, GQA_KV_HEADS, (), GQA_HEAD_DIM, GQA_WIDTH)
            o = _attention(qt, k, vt, heads_per_step=2, dv=GQA_HEAD_DIM, name=nm + "_gqa_attn")
        else:
            w, lam_vecs, g_sub, w_o = dif[ic]
            ic += 1
            lam_init = 0.8 - 0.6 * math.exp(-0.3 * layer)
            qt, k, vt, sg = _prep_call(
                _diff_prep_kernel, nm + "_dif_prep", x, (g_pre, w), (),
                DIFF_HEADS, DIFF_HEADS, (2,), 2 * DIFF_HEAD_DIM, DIFF_WIDTH)
            o = _diff_attention(qt, k, vt, bias_tiles, lam_vecs, g_sub[:, None],
                                lam_init=lam_init, name=nm + "_dif_attn")
        x = _out_call(x, o, sg, w_o, g_post, nm + "_out")
    return x


def kernel(x_prompt, x_sample, norm_pre, norm_post, rel_bias, mla_w_in, mla_g_q, mla_w_uq, mla_g_kv, mla_w_ukv, mla_w_o, gqa_w_in, gqa_g_q, gqa_g_k, gqa_w_o, dif_w_in, dif_lam_q1, dif_lam_k1, dif_lam_q2, dif_lam_k2, dif_g_sub, dif_w_o):
    mla = []
    for i in range(mla_w_in.shape[0]):
        mla.append(_mla_weights(mla_w_in[i], mla_w_uq[i], mla_w_ukv[i])
                   + (mla_g_q[i], mla_g_kv[i], mla_w_o[i].astype(BF16)))
    gqa = [(gqa_w_in[i].astype(BF16), gqa_g_q[i], gqa_g_k[i], gqa_w_o[i].astype(BF16))
           for i in range(gqa_w_in.shape[0])]
    dif = [(dif_w_in[i].astype(BF16),
            jnp.stack([dif_lam_q1[i], dif_lam_k1[i], dif_lam_q2[i], dif_lam_k2[i]]),
            dif_g_sub[i], dif_w_o[i].astype(BF16))
           for i in range(dif_w_in.shape[0])]
    bias_tiles = _bias_tiles(rel_bias)
    y_prompt = _trunk(x_prompt, "p", norm_pre, norm_post, bias_tiles, mla, gqa, dif)
    y_sample = _trunk(x_sample, "s", norm_pre, norm_post, bias_tiles, mla, gqa, dif)
    return (y_prompt, y_sample)
```

```python
import functools
import math

import jax
import jax.numpy as jnp
from jax import lax
from jax.experimental import pallas as pl
from jax.experimental.pallas import tpu as pltpu

D_MODEL = 1024
DEPTH = 4
N_MIXERS = 3
EPS = 1e-6
ROPE_THETA = 10000.0
GRID_W = 64

MLA_HEADS = 16
MLA_Q_LORA = 256
MLA_KV_LORA = 128
MLA_NOPE = 64
MLA_ROPE = 32
MLA_V = 64
MLA_WIDTH = MLA_HEADS * MLA_V

GQA_HEADS = 8
GQA_KV_HEADS = 2
GQA_HEAD_DIM = 128
GQA_REP = GQA_HEADS // GQA_KV_HEADS
GQA_WIDTH = GQA_HEADS * GQA_HEAD_DIM
GQA_KV_WIDTH = GQA_KV_HEADS * GQA_HEAD_DIM

DIFF_HEADS = 8
DIFF_HEAD_DIM = 64
DIFF_WIDTH = DIFF_HEADS * 2 * DIFF_HEAD_DIM

REL_BUCKETS = 32
REL_MAX_DIST = 128

LANES = 128
BF16_SUBLANES = 16
TOKEN_TILE = 512
KEY_TILE = 1024
Q_TILES_PER_STEP = 8
KEY_TILES_PER_TRIP = 4
MAX_STATIC_PAIRS = 8
VMEM_LIMIT = 56 * 1024 * 1024
LOG2E = math.log2(math.e)

SINGLE_BUFFER = pl.Buffered(1)

F32 = jnp.float32
BF16 = jnp.bfloat16


def _cparams(semantics):
    return pltpu.CompilerParams(dimension_semantics=semantics, vmem_limit_bytes=VMEM_LIMIT)


def _rms(x, g):
    return x * lax.rsqrt(jnp.mean(x * x, axis=-1, keepdims=True) + EPS) * g


def _dot(a, b):
    return jnp.dot(a, b, preferred_element_type=F32)


def _ones_row_block(width):
    row = lax.broadcasted_iota(jnp.int32, (BF16_SUBLANES, width), 0)
    return jnp.where(row == 0, 1.0, 0.0).astype(BF16)


def _silu(x):
    return x * (1.0 / (1.0 + jnp.exp(-x)))


def _mla_prep_kernel(x_ref, gpre_ref, w1_ref, gq_ref, wqa_ref, wqb_ref, gkv_ref, wkk_ref, wkv_ref,
                     cos_ref, sin_ref, qt_ref, k_ref, vt_ref, sg_ref):
    tm = x_ref.shape[1]
    h = _rms(x_ref[0], gpre_ref[...]).astype(BF16)
    proj = _dot(h, w1_ref[...])
    o = 0
    q_lat = proj[:, o:o + MLA_Q_LORA]; o += MLA_Q_LORA
    kv_lat = proj[:, o:o + MLA_KV_LORA]; o += MLA_KV_LORA
    kr_a = proj[:, o:o + LANES]; o += LANES
    kr_b = proj[:, o:o + LANES]; o += LANES
    gate = proj[:, o:o + MLA_WIDTH]
    sg_ref[0] = _silu(gate).astype(BF16)

    cos = cos_ref[...]
    sin = sin_ref[...]
    scale = LOG2E / math.sqrt(MLA_NOPE + MLA_ROPE)
    qn = _rms(q_lat, gq_ref[...]).astype(BF16)
    qa = _dot(qn, wqa_ref[...])
    qb = _dot(qn, wqb_ref[...])
    kvn = _rms(kv_lat, gkv_ref[...]).astype(BF16)
    kk = _dot(kvn, wkk_ref[...])
    vv = _dot(kvn, wkv_ref[...])
    k_rot = kr_a * cos + kr_b * sin
    ones_blk = _ones_row_block(tm)
    for hd in range(MLA_HEADS):
        sl = slice(hd * LANES, (hd + 1) * LANES)
        qh = (qa[:, sl] * cos + qb[:, sl] * sin) * scale
        qt_ref[0, hd] = qh.T.astype(BF16)
        k_ref[0, hd] = (kk[:, sl] + k_rot).astype(BF16)
        vt_ref[0, hd, MLA_V:MLA_V + BF16_SUBLANES, :] = ones_blk
    for pr in range(MLA_HEADS // 2):
        vt = vv[:, pr * LANES:(pr + 1) * LANES].T.astype(BF16)
        vt_ref[0, 2 * pr, 0:MLA_V, :] = vt[0:MLA_V]
        vt_ref[0, 2 * pr + 1, 0:MLA_V, :] = vt[MLA_V:2 * MLA_V]


def _gqa_prep_kernel(x_ref, gpre_ref, w_ref, gq_ref, gk_ref, cos_ref, sin_ref,
                     qt_ref, k_ref, vt_ref, sg_ref):
    tm = x_ref.shape[1]
    h = _rms(x_ref[0], gpre_ref[...]).astype(BF16)
    proj = _dot(h, w_ref[...])
    cos = cos_ref[...]
    sin = sin_ref[...]
    scale = LOG2E / math.sqrt(GQA_HEAD_DIM)

    def rot(xh, g):
        xn = _rms(xh, g)
        return xn * cos + pltpu.roll(xn, GQA_HEAD_DIM // 2, axis=1) * sin

    for hd in range(GQA_HEADS):
        qh = rot(proj[:, hd * LANES:(hd + 1) * LANES], gq_ref[...]) * scale
        qt_ref[0, hd] = qh.T.astype(BF16)
    ones_blk = _ones_row_block(tm)
    for g in range(GQA_KV_HEADS):
        ko = GQA_WIDTH + g * LANES
        vo = GQA_WIDTH + GQA_KV_WIDTH + g * LANES
        k_ref[0, g] = rot(proj[:, ko:ko + LANES], gk_ref[...]).astype(BF16)
        vt_ref[0, g, 0:GQA_HEAD_DIM, :] = proj[:, vo:vo + LANES].T.astype(BF16)
        vt_ref[0, g, GQA_HEAD_DIM:GQA_HEAD_DIM + BF16_SUBLANES, :] = ones_blk
    go = GQA_WIDTH + 2 * GQA_KV_WIDTH
    sg_ref[0] = _silu(proj[:, go:go + GQA_WIDTH]).astype(BF16)


def _diff_prep_kernel(x_ref, gpre_ref, w_ref, qt_ref, k_ref, vt_ref, sg_ref):
    tm = x_ref.shape[1]
    h = _rms(x_ref[0], gpre_ref[...]).astype(BF16)
    proj = _dot(h, w_ref[...])
    scale = LOG2E / math.sqrt(DIFF_HEAD_DIM)
    row = lax.broadcasted_iota(jnp.int32, (LANES, tm), 0)
    ones_blk = _ones_row_block(tm)
    for hd in range(DIFF_HEADS):
        sl = slice(hd * LANES, (hd + 1) * LANES)
        qt = (proj[:, sl] * scale).T
        qt_ref[0, hd, 0] = jnp.where(row < DIFF_HEAD_DIM, qt, 0.0).astype(BF16)
        qt_ref[0, hd, 1] = jnp.where(row >= DIFF_HEAD_DIM, qt, 0.0).astype(BF16)
        k_ref[0, hd] = proj[:, DIFF_WIDTH + hd * LANES:DIFF_WIDTH + (hd + 1) * LANES].astype(BF16)
        vo = 2 * DIFF_WIDTH + hd * LANES
        vt_ref[0, hd, 0:LANES, :] = proj[:, vo:vo + LANES].T.astype(BF16)
        vt_ref[0, hd, LANES:LANES + BF16_SUBLANES, :] = ones_blk
    sg_ref[0] = _silu(proj[:, 3 * DIFF_WIDTH:4 * DIFF_WIDTH]).astype(BF16)


def _full(shape):
    return pl.BlockSpec(shape, lambda b, t: (0,) * len(shape))


def _prep_call(kernel, name, x, consts, tables, n_heads, n_kv, qt_inner, dv, sg_width):
    bsz, seq, _ = x.shape
    tm = TOKEN_TILE
    in_specs = [pl.BlockSpec((1, tm, D_MODEL), lambda b, t: (b, t, 0))]
    in_specs += [_full(c.shape) for c in consts]
    in_specs += [pl.BlockSpec((tm, LANES), lambda b, t: (t, 0)) for _ in tables]
    qt_shape = (bsz, n_heads) + qt_inner + (LANES, seq)
    qt_block = (1, n_heads) + qt_inner + (LANES, tm)
    nq = len(qt_shape)
    out_shape = (
        jax.ShapeDtypeStruct(qt_shape, BF16),
        jax.ShapeDtypeStruct((bsz, n_kv, seq, LANES), BF16),
        jax.ShapeDtypeStruct((bsz, n_kv, dv + BF16_SUBLANES, seq), BF16),
        jax.ShapeDtypeStruct((bsz, seq, sg_width), BF16),
    )
    out_specs = (
        pl.BlockSpec(qt_block, lambda b, t: (b,) + (0,) * (nq - 2) + (t,)),
        pl.BlockSpec((1, n_kv, tm, LANES), lambda b, t: (b, 0, t, 0)),
        pl.BlockSpec((1, n_kv, dv + BF16_SUBLANES, tm), lambda b, t: (b, 0, 0, t)),
        pl.BlockSpec((1, tm, sg_width), lambda b, t: (b, t, 0)),
    )
    return pl.pallas_call(
        kernel, grid=(bsz, seq // tm), in_specs=in_specs, out_specs=out_specs, out_shape=out_shape,
        compiler_params=_cparams(("parallel", "parallel")), name=name,
    )(x, *consts, *tables)


def _flash_loop(scores, values, finalize, n_streams, n_q, n_k, sa_ref, sb_ref, tmax_ref, m_ref, acc_ref):
    unroll = min(KEY_TILES_PER_TRIP, n_k)
    static_q = n_q * n_k <= MAX_STATIC_PAIRS
    assert unroll % 2 == 0 and n_k % unroll == 0
    m_ref[...] = jnp.full(m_ref.shape, -jnp.inf, F32)
    acc_ref[...] = jnp.zeros(acc_ref.shape, F32)
    bufs = (sa_ref, sb_ref)
    trips = n_k // unroll

    def fill(st, qi, j, slot):
        s = scores(st, qi, j)
        bufs[slot][st] = s
        tmax_ref[slot, st] = jnp.max(s, axis=0, keepdims=True)

    def consume(st, j, slot):
        m_prev = m_ref[st]
        m_new = jnp.maximum(m_prev, tmax_ref[slot, st])
        p = jnp.exp2(bufs[slot][st] - m_new).astype(BF16)
        alpha = jnp.exp2(m_prev - m_new)
        acc_ref[st] = acc_ref[st] * alpha + _dot(values(st, j), p)
        m_ref[st] = m_new

    def trip(qi, i, last):
        for u in range(unroll):
            j = unroll * i + u
            for st in range(n_streams):
                if not (last and u == unroll - 1):
                    fill(st, qi, j + 1, (u + 1) % 2)
                elif not isinstance(qi, int):
                    fill(st, jnp.minimum(qi + 1, n_q - 1), 0, 0)
                elif qi + 1 < n_q:
                    fill(st, qi + 1, 0, 0)
                consume(st, j, u % 2)

    def query_tile(qi):
        def body(i, carry):
            trip(qi, i, False)
            return carry

        lax.fori_loop(0, trips - 1, body, 0)
        trip(qi, trips - 1, True)
        finalize(qi)
        m_ref[...] = jnp.full(m_ref.shape, -jnp.inf, F32)

    for st in range(n_streams):
        fill(st, 0, 0, 0)
    if static_q:
        for qi in range(n_q):
            query_tile(qi)
    else:
        def q_body(qi, carry):
            query_tile(qi)
            return carry

        lax.fori_loop(0, n_q, q_body, 0)


def _tile_start(j, size):
    return j * size if isinstance(j, int) else pl.multiple_of(j * size, size)


def _attn_kernel(qt_ref, k_ref, vt_ref, o_ref, sa_ref, sb_ref, tmax_ref, m_ref, acc_ref,
                 *, heads, dv, tq, tk):
    shared_kv = k_ref.shape[1] == 1

    def scores(hd, qi, j):
        kv = 0 if shared_kv else hd
        return _dot(k_ref[0, kv, pl.ds(_tile_start(j, tk), tk), :],
                    qt_ref[0, hd, :, pl.ds(_tile_start(qi, tq), tq)])

    def values(hd, j):
        kv = 0 if shared_kv else hd
        return vt_ref[0, kv, :, pl.ds(_tile_start(j, tk), tk)]

    def finalize(qi):
        outs = []
        for hd in range(heads):
            acc = acc_ref[hd]
            outs.append(acc[0:dv] * (1.0 / acc[dv:dv + 1]))
        o_t = outs[0] if heads == 1 else jnp.concatenate(outs, axis=0)
        for c in range(heads * dv // LANES):
            o_ref[0, c, pl.ds(_tile_start(qi, tq), tq), :] = o_t[c * LANES:(c + 1) * LANES].T.astype(BF16)

    _flash_loop(scores, values, finalize, heads, qt_ref.shape[3] // tq, k_ref.shape[2] // tk,
                sa_ref, sb_ref, tmax_ref, m_ref, acc_ref)


def _attention(qt, k, vt, *, heads_per_step, dv, name):
    bsz, n_heads, _, seq = qt.shape
    n_kv = k.shape[1]
    tq = TOKEN_TILE
    tk = KEY_TILE
    tqs = tq * min(Q_TILES_PER_STEP, seq // tq)
    hps = heads_per_step
    rep = n_heads // n_kv
    kv_per_step = 1 if rep > 1 else hps
    assert (rep == 1) or (rep % hps == 0)
    kv_index = (lambda h: (h * hps) // rep) if rep > 1 else (lambda h: h)
    cols = hps * dv // LANES
    dvp = dv + BF16_SUBLANES
    kernel = functools.partial(_attn_kernel, heads=hps, dv=dv, tq=tq, tk=tk)
    return pl.pallas_call(
        kernel,
        grid=(bsz, n_heads // hps, seq // tqs),
        in_specs=[
            pl.BlockSpec((1, hps, LANES, tqs), lambda b, h, q: (b, h, 0, q)),
            pl.BlockSpec((1, kv_per_step, seq, LANES), lambda b, h, q: (b, kv_index(h), 0, 0),
                         pipeline_mode=SINGLE_BUFFER),
            pl.BlockSpec((1, kv_per_step, dvp, seq), lambda b, h, q: (b, kv_index(h), 0, 0),
                         pipeline_mode=SINGLE_BUFFER),
        ],
        out_specs=pl.BlockSpec((1, cols, tqs, LANES), lambda b, h, q: (b, h, q, 0)),
        out_shape=jax.ShapeDtypeStruct((bsz, n_heads * dv // LANES, seq, LANES), BF16),
        scratch_shapes=[pltpu.VMEM((hps, tk, tq), F32), pltpu.VMEM((hps, tk, tq), F32),
                        pltpu.VMEM((2, hps, 1, tq), F32),
                        pltpu.VMEM((hps, 1, tq), F32), pltpu.VMEM((hps, dvp, tq), F32)],
        compiler_params=_cparams(("parallel", "parallel", "arbitrary")), name=name,
    )(qt, k, vt)


def _diff_attn_kernel(lam_ref, gsub_ref, qt_ref, k_ref, vt_ref, bias_ref, o_ref,
                      sa_ref, sb_ref, tmax_ref, m_ref, acc_ref, *, tq, tk, lam_init):
    n_q = qt_ref.shape[4] // tq
    q_base = pl.program_id(2) * n_q
    e_lo, e_hi = _bias_offsets(tk, tq)

    def scores(c, qi, j):
        bias = bias_ref[0, jnp.clip(j * (tk // tq) - (q_base + qi), e_lo, e_hi) - e_lo]
        return _dot(k_ref[0, 0, pl.ds(_tile_start(j, tk), tk), :],
                    qt_ref[0, 0, c, :, pl.ds(_tile_start(qi, tq), tq)]) + bias

    def values(c, j):
        return vt_ref[0, 0, :, pl.ds(_tile_start(j, tk), tk)]

    def finalize(qi):
        lq1, lk1, lq2, lk2 = (lam_ref[i:i + 1, :] for i in range(4))
        lam = (jnp.exp(jnp.sum(lq1 * lk1, axis=1, keepdims=True))
               - jnp.exp(jnp.sum(lq2 * lk2, axis=1, keepdims=True)) + lam_init)
        a1 = acc_ref[0]
        a2 = acc_ref[1]
        o_t = (a1[0:LANES] * (1.0 / a1[LANES:LANES + 1])
               - lam * (a2[0:LANES] * (1.0 / a2[LANES:LANES + 1])))
        ms = jnp.mean(o_t * o_t, axis=0, keepdims=True)
        o_t = o_t * lax.rsqrt(ms + EPS) * gsub_ref[...] * (1.0 - lam_init)
        o_ref[0, 0, pl.ds(_tile_start(qi, tq), tq), :] = o_t.T.astype(BF16)

    _flash_loop(scores, values, finalize, 2, n_q, k_ref.shape[2] // tk,
                sa_ref, sb_ref, tmax_ref, m_ref, acc_ref)


def _diff_attention(qt, k, vt, bias, lam_vecs, g_sub_col, *, lam_init, name):
    bsz, n_heads, _, _, seq = qt.shape
    tq = TOKEN_TILE
    tk = KEY_TILE
    assert bias.shape[2:] == (tk, tq) and tk % tq == 0
    tqs = tq * min(Q_TILES_PER_STEP, seq // tq)
    dvp = LANES + BF16_SUBLANES
    kernel = functools.partial(_diff_attn_kernel, tq=tq, tk=tk, lam_init=lam_init)
    return pl.pallas_call(
        kernel,
        grid=(bsz, n_heads, seq // tqs),
        in_specs=[
            pl.BlockSpec((4, DIFF_HEAD_DIM), lambda b, h, q: (0, 0)),
            pl.BlockSpec((LANES, 1), lambda b, h, q: (0, 0)),
            pl.BlockSpec((1, 1, 2, LANES, tqs), lambda b, h, q: (b, h, 0, 0, q)),
            pl.BlockSpec((1, 1, seq, LANES), lambda b, h, q: (b, h, 0, 0), pipeline_mode=SINGLE_BUFFER),
            pl.BlockSpec((1, 1, dvp, seq), lambda b, h, q: (b, h, 0, 0), pipeline_mode=SINGLE_BUFFER),
            pl.BlockSpec((1,) + bias.shape[1:], lambda b, h, q: (h, 0, 0, 0), pipeline_mode=SINGLE_BUFFER),
        ],
        out_specs=pl.BlockSpec((1, 1, tqs, LANES), lambda b, h, q: (b, h, q, 0)),
        out_shape=jax.ShapeDtypeStruct((bsz, n_heads, seq, LANES), BF16),
        scratch_shapes=[pltpu.VMEM((2, tk, tq), F32), pltpu.VMEM((2, tk, tq), F32),
                        pltpu.VMEM((2, 2, 1, tq), F32),
                        pltpu.VMEM((2, 1, tq), F32), pltpu.VMEM((2, dvp, tq), F32)],
        compiler_params=_cparams(("parallel", "parallel", "arbitrary")), name=name,
    )(lam_vecs, g_sub_col, qt, k, vt, bias)


def _bias_kernel(rel_ref, bucket_ref, o_ref):
    hd = pl.program_id(0)
    chunk = 64

    def body(c, carry):
        rows = pl.ds(pl.multiple_of(c * chunk, chunk), chunk)
        idx = bucket_ref[0, rows, :]
        acc = jnp.zeros(idx.shape, F32)
        for b in range(REL_BUCKETS):
            acc = jnp.where(idx == b, rel_ref[b, hd] * LOG2E, acc)
        o_ref[0, 0, rows, :] = acc
        return carry

    lax.fori_loop(0, bucket_ref.shape[1] // chunk, body, 0)


def _t5_bucket(rel):
    half = REL_BUCKETS // 2
    max_exact = half // 2
    base = (rel > 0).astype(jnp.int32) * half
    n = jnp.abs(rel)
    nf = jnp.maximum(n, 1).astype(F32)
    large = max_exact + (jnp.log(nf / max_exact) / math.log(REL_MAX_DIST / max_exact)
                         * (half - max_exact)).astype(jnp.int32)
    large = jnp.minimum(large, half - 1)
    return base + jnp.where(n < max_exact, n, large)


def _bias_offsets(tk, tq):
    lo = (-REL_MAX_DIST - tk + 1) // tq
    hi = -((-REL_MAX_DIST - tq + 1) // tq)
    return lo, hi


def _bias_tiles(rel_bias, tk, tq):
    lo, hi = _bias_offsets(tk, tq)
    kk = jnp.arange(tk)[:, None]
    qq = jnp.arange(tq)[None, :]
    buckets = jnp.stack([_t5_bucket(e * tq + kk - qq) for e in range(lo, hi + 1)])
    n = hi - lo + 1
    return pl.pallas_call(
        _bias_kernel,
        grid=(DIFF_HEADS, n),
        in_specs=[pl.BlockSpec(memory_space=pltpu.SMEM),
                  pl.BlockSpec((1, tk, tq), lambda h, e: (e, 0, 0))],
        out_specs=pl.BlockSpec((1, 1, tk, tq), lambda h, e: (h, e, 0, 0)),
        out_shape=jax.ShapeDtypeStruct((DIFF_HEADS, n, tk, tq), F32),
        compiler_params=_cparams(("parallel", "parallel")), name="t5_bias_tiles",
    )(rel_bias, buckets)


def _out_kernel(x_ref, o_ref, sg_ref, w_ref, g_ref, y_ref):
    cols = o_ref.shape[1]
    gated = [(o_ref[0, c].astype(F32) * sg_ref[0, :, c * LANES:(c + 1) * LANES].astype(F32)).astype(BF16)
             for c in range(cols)]
    m = _dot(jnp.concatenate(gated, axis=1), w_ref[...])
    y_ref[0] = x_ref[0] + _rms(m, g_ref[...])


def _out_call(x, o, sg, w_o, g_post, name):
    bsz, seq, _ = x.shape
    tm = TOKEN_TILE
    cols = o.shape[1]
    return pl.pallas_call(
        _out_kernel,
        grid=(bsz, seq // tm),
        in_specs=[
            pl.BlockSpec((1, tm, D_MODEL), lambda b, t: (b, t, 0)),
            pl.BlockSpec((1, cols, tm, LANES), lambda b, t: (b, 0, t, 0)),
            pl.BlockSpec((1, tm, cols * LANES), lambda b, t: (b, t, 0)),
            _full(w_o.shape),
            _full(g_post.shape),
        ],
        out_specs=pl.BlockSpec((1, tm, D_MODEL), lambda b, t: (b, t, 0)),
        out_shape=jax.ShapeDtypeStruct(x.shape, F32),
        compiler_params=_cparams(("parallel", "parallel")), name=name,
    )(x, o, sg, w_o, g_post)


def _rope_angles(pos, dim):
    inv = ROPE_THETA ** (-(jnp.arange(0, dim, 2, dtype=F32) / dim))
    return pos.astype(F32)[:, None] * inv[None, :]


def _mla_tables(seq):
    ang = _rope_angles(jnp.arange(seq), MLA_ROPE)
    cos, sin = jnp.cos(ang), jnp.sin(ang)
    pad = LANES - MLA_NOPE - MLA_ROPE
    cos_t = jnp.concatenate([jnp.ones((seq, MLA_NOPE), F32), cos, cos, jnp.zeros((seq, pad), F32)], axis=1)
    sin_t = jnp.concatenate([jnp.zeros((seq, MLA_NOPE), F32), -sin, sin, jnp.zeros((seq, pad), F32)], axis=1)
    return cos_t, sin_t


def _gqa_tables(seq):
    rows = seq // GRID_W
    row = jnp.repeat(jnp.arange(rows), GRID_W)
    col = jnp.tile(jnp.arange(GRID_W), rows)
    ang = jnp.concatenate([_rope_angles(row, GQA_HEAD_DIM // 2), _rope_angles(col, GQA_HEAD_DIM // 2)], axis=-1)
    cos, sin = jnp.cos(ang), jnp.sin(ang)
    return jnp.concatenate([cos, cos], axis=1), jnp.concatenate([-sin, sin], axis=1)


def _mla_weights(w_in, w_uq, w_ukv):
    half = MLA_ROPE // 2
    pad = LANES - MLA_NOPE - MLA_ROPE
    o_kv = MLA_Q_LORA
    o_kr = o_kv + MLA_KV_LORA
    o_gate = o_kr + MLA_ROPE
    w_kr = w_in[:, o_kr:o_gate]
    z = lambda n, rows: jnp.zeros((rows, n), F32)
    kr_a = jnp.concatenate([z(MLA_NOPE, D_MODEL), w_kr, z(pad, D_MODEL)], axis=1)
    kr_b = jnp.concatenate([z(MLA_NOPE, D_MODEL), w_kr[:, half:], w_kr[:, :half], z(pad, D_MODEL)], axis=1)
    w1 = jnp.concatenate([w_in[:, :o_kr], kr_a, kr_b, w_in[:, o_gate:]], axis=1)
    uq = w_uq.reshape(MLA_Q_LORA, MLA_HEADS, MLA_NOPE + MLA_ROPE)
    nope, r1, r2 = uq[..., :MLA_NOPE], uq[..., MLA_NOPE:MLA_NOPE + half], uq[..., MLA_NOPE + half:]
    zq = jnp.zeros((MLA_Q_LORA, MLA_HEADS, pad), F32)
    wq_a = jnp.concatenate([nope, r1, r2, zq], axis=-1).reshape(MLA_Q_LORA, MLA_HEADS * LANES)
    wq_b = jnp.concatenate([jnp.zeros_like(nope), r2, r1, zq], axis=-1).reshape(MLA_Q_LORA, MLA_HEADS * LANES)
    ukv = w_ukv.reshape(MLA_KV_LORA, MLA_HEADS, MLA_NOPE + MLA_V)
    wk = jnp.concatenate([ukv[..., :MLA_NOPE], jnp.zeros((MLA_KV_LORA, MLA_HEADS, LANES - MLA_NOPE), F32)],
                         axis=-1).reshape(MLA_KV_LORA, MLA_HEADS * LANES)
    wv = ukv[..., MLA_NOPE:].reshape(MLA_KV_LORA, MLA_WIDTH)
    return tuple(w.astype(BF16) for w in (w1, wq_a, wq_b, wk, wv))


def _trunk(x, tag, norm_pre, norm_post, bias_tiles, mla, gqa, dif):
    bsz, seq, _ = x.shape
    ia = ib = ic = 0
    for layer in range(DEPTH):
        kind = layer % N_MIXERS
        g_pre = norm_pre[layer][None, :]
        g_post = norm_post[layer][None, :]
        nm = f"{tag}_l{layer}"
        if kind == 0:
            w1, wq_a, wq_b, wk, wv, g_q, g_kv, w_o = mla[ia]
            ia += 1
            qt, k, vt, sg = _prep_call(
                _mla_prep_kernel, nm + "_mla_prep", x,
                (g_pre, w1, g_q[None, :], wq_a, wq_b, g_kv[None, :], wk, wv), _mla_tables(seq),
                MLA_HEADS, MLA_HEADS, (), MLA_V, MLA_WIDTH)
            o = _attention(qt, k, vt, heads_per_step=2, dv=MLA_V, name=nm + "_mla_attn")
        elif kind == 1:
            w, g_q, g_k, w_o = gqa[ib]
            ib += 1
            qt, k, vt, sg = _prep_call(
                _gqa_prep_kernel, nm + "_gqa_prep", x, (g_pre, w, g_q[None, :], g_k[None, :]),
                _gqa_tables(seq), GQA_HEADS, GQA_KV_HEADS, (), GQA_HEAD_DIM, GQA_WIDTH)
            o = _attention(qt, k, vt, heads_per_step=2, dv=GQA_HEAD_DIM, name=nm + "_gqa_attn")
        else:
            w, lam_vecs, g_sub, w_o = dif[ic]
            ic += 1
            lam_init = 0.8 - 0.6 * math.exp(-0.3 * layer)
            qt, k, vt, sg = _prep_call(
                _diff_prep_kernel, nm + "_dif_prep", x, (g_pre, w), (),
                DIFF_HEADS, DIFF_HEADS, (2,), 2 * DIFF_HEAD_DIM, DIFF_WIDTH)
            o = _diff_attention(qt, k, vt, bias_tiles, lam_vecs, g_sub[:, None],
                                lam_init=lam_init, name=nm + "_dif_attn")
        x = _out_call(x, o, sg, w_o, g_post, nm + "_out")
    return x


def kernel(x_prompt, x_sample, norm_pre, norm_post, rel_bias, mla_w_in, mla_g_q, mla_w_uq, mla_g_kv, mla_w_ukv, mla_w_o, gqa_w_in, gqa_g_q, gqa_g_k, gqa_w_o, dif_w_in, dif_lam_q1, dif_lam_k1, dif_lam_q2, dif_lam_k2, dif_g_sub, dif_w_o):
    mla = []
    for i in range(mla_w_in.shape[0]):
        mla.append(_mla_weights(mla_w_in[i], mla_w_uq[i], mla_w_ukv[i])
                   + (mla_g_q[i], mla_g_kv[i], mla_w_o[i].astype(BF16)))
    gqa = [(gqa_w_in[i].astype(BF16), gqa_g_q[i], gqa_g_k[i], gqa_w_o[i].astype(BF16))
           for i in range(gqa_w_in.shape[0])]
    dif = [(dif_w_in[i].astype(BF16),
            jnp.stack([dif_lam_q1[i], dif_lam_k1[i], dif_lam_q2[i], dif_lam_k2[i]]),
            dif_g_sub[i], dif_w_o[i].astype(BF16))
           for i in range(dif_w_in.shape[0])]
    bias_tiles = _bias_tiles(rel_bias, KEY_TILE, TOKEN_TILE)
    y_prompt = _trunk(x_prompt, "p", norm_pre, norm_post, bias_tiles, mla, gqa, dif)
    y_sample = _trunk(x_sample, "s", norm_pre, norm_post, bias_tiles, mla, gqa, dif)
    return (y_prompt, y_sample)
```

```python
import functools
import math

import jax
import jax.numpy as jnp
from jax import lax
from jax.experimental import pallas as pl
from jax.experimental.pallas import tpu as pltpu

D_MODEL = 1024
DEPTH = 4
N_MIXERS = 3
EPS = 1e-6
ROPE_THETA = 10000.0
GRID_W = 64

MLA_HEADS = 16
MLA_Q_LORA = 256
MLA_KV_LORA = 128
MLA_NOPE = 64
MLA_ROPE = 32
MLA_V = 64
MLA_WIDTH = MLA_HEADS * MLA_V

GQA_HEADS = 8
GQA_KV_HEADS = 2
GQA_HEAD_DIM = 128
GQA_REP = GQA_HEADS // GQA_KV_HEADS
GQA_WIDTH = GQA_HEADS * GQA_HEAD_DIM
GQA_KV_WIDTH = GQA_KV_HEADS * GQA_HEAD_DIM

DIFF_HEADS = 8
DIFF_HEAD_DIM = 64
DIFF_WIDTH = DIFF_HEADS * 2 * DIFF_HEAD_DIM

REL_BUCKETS = 32
REL_MAX_DIST = 128

LANES = 128
BF16_SUBLANES = 16
TOKEN_TILE = 512
KEY_TILE = 1024
Q_TILES_PER_STEP = 8
KEY_TILES_PER_TRIP = 4
MAX_STATIC_PAIRS = 8
VMEM_LIMIT = 56 * 1024 * 1024
LOG2E = math.log2(math.e)

SINGLE_BUFFER = pl.Buffered(1)

F32 = jnp.float32
BF16 = jnp.bfloat16


def _cparams(semantics):
    return pltpu.CompilerParams(dimension_semantics=semantics, vmem_limit_bytes=VMEM_LIMIT)


def _rms(x, g):
    return x * lax.rsqrt(jnp.mean(x * x, axis=-1, keepdims=True) + EPS) * g


def _dot(a, b):
    return jnp.dot(a, b, preferred_element_type=F32)


def _ones_row_block(width):
    row = lax.broadcasted_iota(jnp.int32, (BF16_SUBLANES, width), 0)
    return jnp.where(row == 0, 1.0, 0.0).astype(BF16)


def _silu(x):
    return x * (1.0 / (1.0 + jnp.exp(-x)))


def _mla_prep_kernel(x_ref, gpre_ref, w1_ref, gq_ref, wqa_ref, wqb_ref, gkv_ref, wkk_ref, wkv_ref,
                     cos_ref, sin_ref, qt_ref, k_ref, vt_ref, sg_ref):
    tm = x_ref.shape[1]
    h = _rms(x_ref[0], gpre_ref[...]).astype(BF16)
    proj = _dot(h, w1_ref[...])
    o = 0
    q_lat = proj[:, o:o + MLA_Q_LORA]; o += MLA_Q_LORA
    kv_lat = proj[:, o:o + MLA_KV_LORA]; o += MLA_KV_LORA
    kr_a = proj[:, o:o + LANES]; o += LANES
    kr_b = proj[:, o:o + LANES]; o += LANES
    gate = proj[:, o:o + MLA_WIDTH]
    sg_ref[0] = _silu(gate).astype(BF16)

    cos = cos_ref[...]
    sin = sin_ref[...]
    scale = LOG2E / math.sqrt(MLA_NOPE + MLA_ROPE)
    qn = _rms(q_lat, gq_ref[...]).astype(BF16)
    qa = _dot(qn, wqa_ref[...])
    qb = _dot(qn, wqb_ref[...])
    kvn = _rms(kv_lat, gkv_ref[...]).astype(BF16)
    kk = _dot(kvn, wkk_ref[...])
    vv = _dot(kvn, wkv_ref[...])
    k_rot = kr_a * cos + kr_b * sin
    ones_blk = _ones_row_block(tm)
    for hd in range(MLA_HEADS):
        sl = slice(hd * LANES, (hd + 1) * LANES)
        qh = (qa[:, sl] * cos + qb[:, sl] * sin) * scale
        qt_ref[0, hd] = qh.T.astype(BF16)
        k_ref[0, hd] = (kk[:, sl] + k_rot).astype(BF16)
        vt_ref[0, hd, MLA_V:MLA_V + BF16_SUBLANES, :] = ones_blk
    for pr in range(MLA_HEADS // 2):
        vt = vv[:, pr * LANES:(pr + 1) * LANES].T.astype(BF16)
        vt_ref[0, 2 * pr, 0:MLA_V, :] = vt[0:MLA_V]
        vt_ref[0, 2 * pr + 1, 0:MLA_V, :] = vt[MLA_V:2 * MLA_V]


def _gqa_prep_kernel(x_ref, gpre_ref, w_ref, gq_ref, gk_ref, cos_ref, sin_ref,
                     qt_ref, k_ref, vt_ref, sg_ref):
    tm = x_ref.shape[1]
    h = _rms(x_ref[0], gpre_ref[...]).astype(BF16)
    proj = _dot(h, w_ref[...])
    cos = cos_ref[...]
    sin = sin_ref[...]
    scale = LOG2E / math.sqrt(GQA_HEAD_DIM)

    def rot(xh, g):
        xn = _rms(xh, g)
        return xn * cos + pltpu.roll(xn, GQA_HEAD_DIM // 2, axis=1) * sin

    for hd in range(GQA_HEADS):
        qh = rot(proj[:, hd * LANES:(hd + 1) * LANES], gq_ref[...]) * scale
        qt_ref[0, hd] = qh.T.astype(BF16)
    ones_blk = _ones_row_block(tm)
    for g in range(GQA_KV_HEADS):
        ko = GQA_WIDTH + g * LANES
        vo = GQA_WIDTH + GQA_KV_WIDTH + g * LANES
        k_ref[0, g] = rot(proj[:, ko:ko + LANES], gk_ref[...]).astype(BF16)
        vt_ref[0, g, 0:GQA_HEAD_DIM, :] = proj[:, vo:vo + LANES].T.astype(BF16)
        vt_ref[0, g, GQA_HEAD_DIM:GQA_HEAD_DIM + BF16_SUBLANES, :] = ones_blk
    go = GQA_WIDTH + 2 * GQA_KV_WIDTH
    sg_ref[0] = _silu(proj[:, go:go + GQA_WIDTH]).astype(BF16)


def _diff_prep_kernel(x_ref, gpre_ref, w_ref, qt_ref, k_ref, vt_ref, sg_ref):
    tm = x_ref.shape[1]
    h = _rms(x_ref[0], gpre_ref[...]).astype(BF16)
    proj = _dot(h, w_ref[...])
    scale = LOG2E / math.sqrt(DIFF_HEAD_DIM)
    row = lax.broadcasted_iota(jnp.int32, (LANES, tm), 0)
    ones_blk = _ones_row_block(tm)
    for hd in range(DIFF_HEADS):
        sl = slice(hd * LANES, (hd + 1) * LANES)
        qt = (proj[:, sl] * scale).T
        qt_ref[0, hd, 0] = jnp.where(row < DIFF_HEAD_DIM, qt, 0.0).astype(BF16)
        qt_ref[0, hd, 1] = jnp.where(row >= DIFF_HEAD_DIM, qt, 0.0).astype(BF16)
        k_ref[0, hd] = proj[:, DIFF_WIDTH + hd * LANES:DIFF_WIDTH + (hd + 1) * LANES].astype(BF16)
        vo = 2 * DIFF_WIDTH + hd * LANES
        vt_ref[0, hd, 0:LANES, :] = proj[:, vo:vo + LANES].T.astype(BF16)
        vt_ref[0, hd, LANES:LANES + BF16_SUBLANES, :] = ones_blk
    sg_ref[0] = _silu(proj[:, 3 * DIFF_WIDTH:4 * DIFF_WIDTH]).astype(BF16)


def _full(shape):
    return pl.BlockSpec(shape, lambda b, t: (0,) * len(shape))


def _prep_call(kernel, name, x, consts, tables, n_heads, n_kv, qt_inner, dv, sg_width):
    bsz, seq, _ = x.shape
    tm = TOKEN_TILE
    in_specs = [pl.BlockSpec((1, tm, D_MODEL), lambda b, t: (b, t, 0))]
    in_specs += [_full(c.shape) for c in consts]
    in_specs += [pl.BlockSpec((tm, LANES), lambda b, t: (t, 0)) for _ in tables]
    qt_shape = (bsz, n_heads) + qt_inner + (LANES, seq)
    qt_block = (1, n_heads) + qt_inner + (LANES, tm)
    nq = len(qt_shape)
    out_shape = (
        jax.ShapeDtypeStruct(qt_shape, BF16),
        jax.ShapeDtypeStruct((bsz, n_kv, seq, LANES), BF16),
        jax.ShapeDtypeStruct((bsz, n_kv, dv + BF16_SUBLANES, seq), BF16),
        jax.ShapeDtypeStruct((bsz, seq, sg_width), BF16),
    )
    out_specs = (
        pl.BlockSpec(qt_block, lambda b, t: (b,) + (0,) * (nq - 2) + (t,)),
        pl.BlockSpec((1, n_kv, tm, LANES), lambda b, t: (b, 0, t, 0)),
        pl.BlockSpec((1, n_kv, dv + BF16_SUBLANES, tm), lambda b, t: (b, 0, 0, t)),
        pl.BlockSpec((1, tm, sg_width), lambda b, t: (b, t, 0)),
    )
    return pl.pallas_call(
        kernel, grid=(bsz, seq // tm), in_specs=in_specs, out_specs=out_specs, out_shape=out_shape,
        compiler_params=_cparams(("parallel", "parallel")), name=name,
    )(x, *consts, *tables)


def _flash_loop(scores, values, finalize, n_streams, n_q, n_k, sa_ref, sb_ref, tmax_ref, m_ref, acc_ref):
    unroll = min(KEY_TILES_PER_TRIP, n_k)
    static_q = n_q * n_k <= MAX_STATIC_PAIRS
    assert unroll % 2 == 0 and n_k % unroll == 0
    m_ref[...] = jnp.full(m_ref.shape, -jnp.inf, F32)
    acc_ref[...] = jnp.zeros(acc_ref.shape, F32)
    bufs = (sa_ref, sb_ref)
    trips = n_k // unroll

    def fill(st, qi, j, slot):
        s = scores(st, qi, j)
        bufs[slot][st] = s
        tmax_ref[slot, st] = jnp.max(s, axis=0, keepdims=True)

    def consume(st, j, slot):
        m_prev = m_ref[st]
        m_new = jnp.maximum(m_prev, tmax_ref[slot, st])
        p = jnp.exp2(bufs[slot][st] - m_new).astype(BF16)
        alpha = jnp.exp2(m_prev - m_new)
        acc_ref[st] = acc_ref[st] * alpha + _dot(values(st, j), p)
        m_ref[st] = m_new

    def trip(qi, i, last):
        for u in range(unroll):
            j = unroll * i + u
            for st in range(n_streams):
                if not (last and u == unroll - 1):
                    fill(st, qi, j + 1, (u + 1) % 2)
                elif not isinstance(qi, int):
                    fill(st, jnp.minimum(qi + 1, n_q - 1), 0, 0)
                elif qi + 1 < n_q:
                    fill(st, qi + 1, 0, 0)
                consume(st, j, u % 2)

    def query_tile(qi):
        def body(i, carry):
            trip(qi, i, False)
            return carry

        lax.fori_loop(0, trips - 1, body, 0)
        trip(qi, trips - 1, True)
        finalize(qi)
        m_ref[...] = jnp.full(m_ref.shape, -jnp.inf, F32)

    for st in range(n_streams):
        fill(st, 0, 0, 0)
    if static_q:
        for qi in range(n_q):
            query_tile(qi)
    else:
        def q_body(qi, carry):
            query_tile(qi)
            return carry

        lax.fori_loop(0, n_q, q_body, 0)


def _tile_start(j, size):
    return j * size if isinstance(j, int) else pl.multiple_of(j * size, size)


def _attn_kernel(qt_ref, k_ref, vt_ref, o_ref, sa_ref, sb_ref, tmax_ref, m_ref, acc_ref,
                 *, heads, dv, tq, tk):
    shared_kv = k_ref.shape[1] == 1

    def scores(hd, qi, j):
        kv = 0 if shared_kv else hd
        return _dot(k_ref[0, kv, pl.ds(_tile_start(j, tk), tk), :],
                    qt_ref[0, hd, :, pl.ds(_tile_start(qi, tq), tq)])

    def values(hd, j):
        kv = 0 if shared_kv else hd
        return vt_ref[0, kv, :, pl.ds(_tile_start(j, tk), tk)]

    def finalize(qi):
        outs = []
        for hd in range(heads):
            acc = acc_ref[hd]
            outs.append(acc[0:dv] * (1.0 / acc[dv:dv + 1]))
        o_t = outs[0] if heads == 1 else jnp.concatenate(outs, axis=0)
        for c in range(heads * dv // LANES):
            o_ref[0, c, pl.ds(_tile_start(qi, tq), tq), :] = o_t[c * LANES:(c + 1) * LANES].T.astype(BF16)

    _flash_loop(scores, values, finalize, heads, qt_ref.shape[3] // tq, k_ref.shape[2] // tk,
                sa_ref, sb_ref, tmax_ref, m_ref, acc_ref)


def _attention(qt, k, vt, *, heads_per_step, dv, name):
    bsz, n_heads, _, seq = qt.shape
    n_kv = k.shape[1]
    tq = TOKEN_TILE
    tk = KEY_TILE
    tqs = tq * min(Q_TILES_PER_STEP, seq // tq)
    hps = heads_per_step
    rep = n_heads // n_kv
    kv_per_step = 1 if rep > 1 else hps
    assert (rep == 1) or (rep % hps == 0)
    kv_index = (lambda h: (h * hps) // rep) if rep > 1 else (lambda h: h)
    cols = hps * dv // LANES
    dvp = dv + BF16_SUBLANES
    kernel = functools.partial(_attn_kernel, heads=hps, dv=dv, tq=tq, tk=tk)
    return pl.pallas_call(
        kernel,
        grid=(bsz, n_heads // hps, seq // tqs),
        in_specs=[
            pl.BlockSpec((1, hps, LANES, tqs), lambda b, h, q: (b, h, 0, q)),
            pl.BlockSpec((1, kv_per_step, seq, LANES), lambda b, h, q: (b, kv_index(h), 0, 0),
                         pipeline_mode=SINGLE_BUFFER),
            pl.BlockSpec((1, kv_per_step, dvp, seq), lambda b, h, q: (b, kv_index(h), 0, 0),
                         pipeline_mode=SINGLE_BUFFER),
        ],
        out_specs=pl.BlockSpec((1, cols, tqs, LANES), lambda b, h, q: (b, h, q, 0)),
        out_shape=jax.ShapeDtypeStruct((bsz, n_heads * dv // LANES, seq, LANES), BF16),
        scratch_shapes=[pltpu.VMEM((hps, tk, tq), F32), pltpu.VMEM((hps, tk, tq), F32),
                        pltpu.VMEM((2, hps, 1, tq), F32),
                        pltpu.VMEM((hps, 1, tq), F32), pltpu.VMEM((hps, dvp, tq), F32)],
        compiler_params=_cparams(("parallel", "parallel", "arbitrary")), name=name,
    )(qt, k, vt)


def _diff_attn_kernel(lam_ref, gsub_ref, qt_ref, k_ref, vt_ref, bias_ref, o_ref,
                      sa_ref, sb_ref, tmax_ref, m_ref, acc_ref, *, tq, tk, lam_init):
    n_q = qt_ref.shape[4] // tq
    q_base = pl.program_id(2) * n_q
    e_lo, e_hi = _bias_offsets(tk, tq)

    def scores(c, qi, j):
        bias = bias_ref[0, jnp.clip(j * (tk // tq) - (q_base + qi), e_lo, e_hi) - e_lo]
        return _dot(k_ref[0, 0, pl.ds(_tile_start(j, tk), tk), :],
                    qt_ref[0, 0, c, :, pl.ds(_tile_start(qi, tq), tq)]) + bias

    def values(c, j):
        return vt_ref[0, 0, :, pl.ds(_tile_start(j, tk), tk)]

    def finalize(qi):
        lq1, lk1, lq2, lk2 = (lam_ref[i:i + 1, :] for i in range(4))
        lam = (jnp.exp(jnp.sum(lq1 * lk1, axis=1, keepdims=True))
               - jnp.exp(jnp.sum(lq2 * lk2, axis=1, keepdims=True)) + lam_init)
        a1 = acc_ref[0]
        a2 = acc_ref[1]
        o_t = (a1[0:LANES] * (1.0 / a1[LANES:LANES + 1])
               - lam * (a2[0:LANES] * (1.0 / a2[LANES:LANES + 1])))
        ms = jnp.mean(o_t * o_t, axis=0, keepdims=True)
        o_t = o_t * lax.rsqrt(ms + EPS) * gsub_ref[...] * (1.0 - lam_init)
        o_ref[0, 0, pl.ds(_tile_start(qi, tq), tq), :] = o_t.T.astype(BF16)

    _flash_loop(scores, values, finalize, 2, n_q, k_ref.shape[2] // tk,
                sa_ref, sb_ref, tmax_ref, m_ref, acc_ref)


def _biased_key_tile(seq):
    return KEY_TILE if seq // KEY_TILE >= KEY_TILES_PER_TRIP else TOKEN_TILE


def _diff_attention(qt, k, vt, bias, lam_vecs, g_sub_col, *, lam_init, name):
    bsz, n_heads, _, _, seq = qt.shape
    tq = TOKEN_TILE
    tk = _biased_key_tile(seq)
    assert bias.shape[2:] == (tk, tq) and tk % tq == 0
    tqs = tq * min(Q_TILES_PER_STEP, seq // tq)
    dvp = LANES + BF16_SUBLANES
    kernel = functools.partial(_diff_attn_kernel, tq=tq, tk=tk, lam_init=lam_init)
    return pl.pallas_call(
        kernel,
        grid=(bsz, n_heads, seq // tqs),
        in_specs=[
            pl.BlockSpec((4, DIFF_HEAD_DIM), lambda b, h, q: (0, 0)),
            pl.BlockSpec((LANES, 1), lambda b, h, q: (0, 0)),
            pl.BlockSpec((1, 1, 2, LANES, tqs), lambda b, h, q: (b, h, 0, 0, q)),
            pl.BlockSpec((1, 1, seq, LANES), lambda b, h, q: (b, h, 0, 0), pipeline_mode=SINGLE_BUFFER),
            pl.BlockSpec((1, 1, dvp, seq), lambda b, h, q: (b, h, 0, 0), pipeline_mode=SINGLE_BUFFER),
            pl.BlockSpec((1,) + bias.shape[1:], lambda b, h, q: (h, 0, 0, 0), pipeline_mode=SINGLE_BUFFER),
        ],
        out_specs=pl.BlockSpec((1, 1, tqs, LANES), lambda b, h, q: (b, h, q, 0)),
        out_shape=jax.ShapeDtypeStruct((bsz, n_heads, seq, LANES), BF16),
        scratch_shapes=[pltpu.VMEM((2, tk, tq), F32), pltpu.VMEM((2, tk, tq), F32),
                        pltpu.VMEM((2, 2, 1, tq), F32),
                        pltpu.VMEM((2, 1, tq), F32), pltpu.VMEM((2, dvp, tq), F32)],
        compiler_params=_cparams(("parallel", "parallel", "arbitrary")), name=name,
    )(lam_vecs, g_sub_col, qt, k, vt, bias)


def _bias_kernel(rel_ref, bucket_ref, o_ref):
    hd = pl.program_id(0)
    chunk = 64

    def body(c, carry):
        rows = pl.ds(pl.multiple_of(c * chunk, chunk), chunk)
        idx = bucket_ref[0, rows, :]
        acc = jnp.zeros(idx.shape, F32)
        for b in range(REL_BUCKETS):
            acc = jnp.where(idx == b, rel_ref[b, hd] * LOG2E, acc)
        o_ref[0, 0, rows, :] = acc
        return carry

    lax.fori_loop(0, bucket_ref.shape[1] // chunk, body, 0)


def _t5_bucket(rel):
    half = REL_BUCKETS // 2
    max_exact = half // 2
    base = (rel > 0).astype(jnp.int32) * half
    n = jnp.abs(rel)
    nf = jnp.maximum(n, 1).astype(F32)
    large = max_exact + (jnp.log(nf / max_exact) / math.log(REL_MAX_DIST / max_exact)
                         * (half - max_exact)).astype(jnp.int32)
    large = jnp.minimum(large, half - 1)
    return base + jnp.where(n < max_exact, n, large)


def _bias_offsets(tk, tq):
    lo = (-REL_MAX_DIST - tk + 1) // tq
    hi = -((-REL_MAX_DIST - tq + 1) // tq)
    return lo, hi


def _bias_tiles(rel_bias, tk, tq):
    lo, hi = _bias_offsets(tk, tq)
    kk = jnp.arange(tk)[:, None]
    qq = jnp.arange(tq)[None, :]
    buckets = jnp.stack([_t5_bucket(e * tq + kk - qq) for e in range(lo, hi + 1)])
    n = hi - lo + 1
    return pl.pallas_call(
        _bias_kernel,
        grid=(DIFF_HEADS, n),
        in_specs=[pl.BlockSpec(memory_space=pltpu.SMEM),
                  pl.BlockSpec((1, tk, tq), lambda h, e: (e, 0, 0))],
        out_specs=pl.BlockSpec((1, 1, tk, tq), lambda h, e: (h, e, 0, 0)),
        out_shape=jax.ShapeDtypeStruct((DIFF_HEADS, n, tk, tq), F32),
        compiler_params=_cparams(("parallel", "parallel")), name=f"t5_bias_tiles_{tk}",
    )(rel_bias, buckets)


def _out_kernel(x_ref, o_ref, sg_ref, w_ref, g_ref, y_ref):
    cols = o_ref.shape[1]
    gated = [(o_ref[0, c].astype(F32) * sg_ref[0, :, c * LANES:(c + 1) * LANES].astype(F32)).astype(BF16)
             for c in range(cols)]
    m = _dot(jnp.concatenate(gated, axis=1), w_ref[...])
    y_ref[0] = x_ref[0] + _rms(m, g_ref[...])


def _out_call(x, o, sg, w_o, g_post, name):
    bsz, seq, _ = x.shape
    tm = TOKEN_TILE
    cols = o.shape[1]
    return pl.pallas_call(
        _out_kernel,
        grid=(bsz, seq // tm),
        in_specs=[
            pl.BlockSpec((1, tm, D_MODEL), lambda b, t: (b, t, 0)),
            pl.BlockSpec((1, cols, tm, LANES), lambda b, t: (b, 0, t, 0)),
            pl.BlockSpec((1, tm, cols * LANES), lambda b, t: (b, t, 0)),
            _full(w_o.shape),
            _full(g_post.shape),
        ],
        out_specs=pl.BlockSpec((1, tm, D_MODEL), lambda b, t: (b, t, 0)),
        out_shape=jax.ShapeDtypeStruct(x.shape, F32),
        compiler_params=_cparams(("parallel", "parallel")), name=name,
    )(x, o, sg, w_o, g_post)


def _rope_angles(pos, dim):
    inv = ROPE_THETA ** (-(jnp.arange(0, dim, 2, dtype=F32) / dim))
    return pos.astype(F32)[:, None] * inv[None, :]


def _mla_tables(seq):
    ang = _rope_angles(jnp.arange(seq), MLA_ROPE)
    cos, sin = jnp.cos(ang), jnp.sin(ang)
    pad = LANES - MLA_NOPE - MLA_ROPE
    cos_t = jnp.concatenate([jnp.ones((seq, MLA_NOPE), F32), cos, cos, jnp.zeros((seq, pad), F32)], axis=1)
    sin_t = jnp.concatenate([jnp.zeros((seq, MLA_NOPE), F32), -sin, sin, jnp.zeros((seq, pad), F32)], axis=1)
    return cos_t, sin_t


def _gqa_tables(seq):
    rows = seq // GRID_W
    row = jnp.repeat(jnp.arange(rows), GRID_W)
    col = jnp.tile(jnp.arange(GRID_W), rows)
    ang = jnp.concatenate([_rope_angles(row, GQA_HEAD_DIM // 2), _rope_angles(col, GQA_HEAD_DIM // 2)], axis=-1)
    cos, sin = jnp.cos(ang), jnp.sin(ang)
    return jnp.concatenate([cos, cos], axis=1), jnp.concatenate([-sin, sin], axis=1)


def _mla_weights(w_in, w_uq, w_ukv):
    half = MLA_ROPE // 2
    pad = LANES - MLA_NOPE - MLA_ROPE
    o_kv = MLA_Q_LORA
    o_kr = o_kv + MLA_KV_LORA
    o_gate = o_kr + MLA_ROPE
    w_kr = w_in[:, o_kr:o_gate]
    z = lambda n, rows: jnp.zeros((rows, n), F32)
    kr_a = jnp.concatenate([z(MLA_NOPE, D_MODEL), w_kr, z(pad, D_MODEL)], axis=1)
    kr_b = jnp.concatenate([z(MLA_NOPE, D_MODEL), w_kr[:, half:], w_kr[:, :half], z(pad, D_MODEL)], axis=1)
    w1 = jnp.concatenate([w_in[:, :o_kr], kr_a, kr_b, w_in[:, o_gate:]], axis=1)
    uq = w_uq.reshape(MLA_Q_LORA, MLA_HEADS, MLA_NOPE + MLA_ROPE)
    nope, r1, r2 = uq[..., :MLA_NOPE], uq[..., MLA_NOPE:MLA_NOPE + half], uq[..., MLA_NOPE + half:]
    zq = jnp.zeros((MLA_Q_LORA, MLA_HEADS, pad), F32)
    wq_a = jnp.concatenate([nope, r1, r2, zq], axis=-1).reshape(MLA_Q_LORA, MLA_HEADS * LANES)
    wq_b = jnp.concatenate([jnp.zeros_like(nope), r2, r1, zq], axis=-1).reshape(MLA_Q_LORA, MLA_HEADS * LANES)
    ukv = w_ukv.reshape(MLA_KV_LORA, MLA_HEADS, MLA_NOPE + MLA_V)
    wk = jnp.concatenate([ukv[..., :MLA_NOPE], jnp.zeros((MLA_KV_LORA, MLA_HEADS, LANES - MLA_NOPE), F32)],
                         axis=-1).reshape(MLA_KV_LORA, MLA_HEADS * LANES)
    wv = ukv[..., MLA_NOPE:].reshape(MLA_KV_LORA, MLA_WIDTH)
    return tuple(w.astype(BF16) for w in (w1, wq_a, wq_b, wk, wv))


def _trunk(x, tag, norm_pre, norm_post, bias_tiles, mla, gqa, dif):
    bsz, seq, _ = x.shape
    ia = ib = ic = 0
    for layer in range(DEPTH):
        kind = layer % N_MIXERS
        g_pre = norm_pre[layer][None, :]
        g_post = norm_post[layer][None, :]
        nm = f"{tag}_l{layer}"
        if kind == 0:
            w1, wq_a, wq_b, wk, wv, g_q, g_kv, w_o = mla[ia]
            ia += 1
            qt, k, vt, sg = _prep_call(
                _mla_prep_kernel, nm + "_mla_prep", x,
                (g_pre, w1, g_q[None, :], wq_a, wq_b, g_kv[None, :], wk, wv), _mla_tables(seq),
                MLA_HEADS, MLA_HEADS, (), MLA_V, MLA_WIDTH)
            o = _attention(qt, k, vt, heads_per_step=2, dv=MLA_V, name=nm + "_mla_attn")
        elif kind == 1:
            w, g_q, g_k, w_o = gqa[ib]
            ib += 1
            qt, k, vt, sg = _prep_call(
                _gqa_prep_kernel, nm + "_gqa_prep", x, (g_pre, w, g_q[None, :], g_k[None, :]),
                _gqa_tables(seq), GQA_HEADS, GQA_KV_HEADS, (), GQA_HEAD_DIM, GQA_WIDTH)
            o = _attention(qt, k, vt, heads_per_step=2, dv=GQA_HEAD_DIM, name=nm + "_gqa_attn")
        else:
            w, lam_vecs, g_sub, w_o = dif[ic]
            ic += 1
            lam_init = 0.8 - 0.6 * math.exp(-0.3 * layer)
            qt, k, vt, sg = _prep_call(
                _diff_prep_kernel, nm + "_dif_prep", x, (g_pre, w), (),
                DIFF_HEADS, DIFF_HEADS, (2,), 2 * DIFF_HEAD_DIM, DIFF_WIDTH)
            o = _diff_attention(qt, k, vt, bias_tiles[_biased_key_tile(seq)], lam_vecs, g_sub[:, None],
                                lam_init=lam_init, name=nm + "_dif_attn")
        x = _out_call(x, o, sg, w_o, g_post, nm + "_out")
    return x


def kernel(x_prompt, x_sample, norm_pre, norm_post, rel_bias, mla_w_in, mla_g_q, mla_w_uq, mla_g_kv, mla_w_ukv, mla_w_o, gqa_w_in, gqa_g_q, gqa_g_k, gqa_w_o, dif_w_in, dif_lam_q1, dif_lam_k1, dif_lam_q2, dif_lam_k2, dif_g_sub, dif_w_o):
    mla = []
    for i in range(mla_w_in.shape[0]):
        mla.append(_mla_weights(mla_w_in[i], mla_w_uq[i], mla_w_ukv[i])
                   + (mla_g_q[i], mla_g_kv[i], mla_w_o[i].astype(BF16)))
    gqa = [(gqa_w_in[i].astype(BF16), gqa_g_q[i], gqa_g_k[i], gqa_w_o[i].astype(BF16))
           for i in range(gqa_w_in.shape[0])]
    dif = [(dif_w_in[i].astype(BF16),
            jnp.stack([dif_lam_q1[i], dif_lam_k1[i], dif_lam_q2[i], dif_lam_k2[i]]),
            dif_g_sub[i], dif_w_o[i].astype(BF16))
           for i in range(dif_w_in.shape[0])]
    bias_tiles = {tk: _bias_tiles(rel_bias, tk, TOKEN_TILE)
                  for tk in sorted({_biased_key_tile(x.shape[1]) for x in (x_prompt, x_sample)})}
    y_prompt = _trunk(x_prompt, "p", norm_pre, norm_post, bias_tiles, mla, gqa, dif)
    y_sample = _trunk(x_sample, "s", norm_pre, norm_post, bias_tiles, mla, gqa, dif)
    return (y_prompt, y_sample)
```

```python
import functools
import math

import jax
import jax.numpy as jnp
from jax import lax
from jax.experimental import pallas as pl
from jax.experimental.pallas import tpu as pltpu

D_MODEL = 1024
DEPTH = 4
N_MIXERS = 3
EPS = 1e-6
ROPE_THETA = 10000.0
GRID_W = 64

MLA_HEADS = 16
MLA_Q_LORA = 256
MLA_KV_LORA = 128
MLA_NOPE = 64
MLA_ROPE = 32
MLA_V = 64
MLA_WIDTH = MLA_HEADS * MLA_V

GQA_HEADS = 8
GQA_KV_HEADS = 2
GQA_HEAD_DIM = 128
GQA_REP = GQA_HEADS // GQA_KV_HEADS
GQA_WIDTH = GQA_HEADS * GQA_HEAD_DIM
GQA_KV_WIDTH = GQA_KV_HEADS * GQA_HEAD_DIM

DIFF_HEADS = 8
DIFF_HEAD_DIM = 64
DIFF_WIDTH = DIFF_HEADS * 2 * DIFF_HEAD_DIM

REL_BUCKETS = 32
REL_MAX_DIST = 128

LANES = 128
BF16_SUBLANES = 16
TOKEN_TILE = 512
KEY_TILE = 1024
Q_TILES_PER_STEP = 8
KEY_TILES_PER_TRIP = 4
MAX_STATIC_PAIRS = 8
VMEM_LIMIT = 56 * 1024 * 1024
LOG2E = math.log2(math.e)

SINGLE_BUFFER = pl.Buffered(1)

F32 = jnp.float32
BF16 = jnp.bfloat16


def _cparams(semantics):
    return pltpu.CompilerParams(dimension_semantics=semantics, vmem_limit_bytes=VMEM_LIMIT)


def _rms(x, g):
    return x * lax.rsqrt(jnp.mean(x * x, axis=-1, keepdims=True) + EPS) * g


def _dot(a, b):
    return jnp.dot(a, b, preferred_element_type=F32)


def _ones_row_block(width):
    row = lax.broadcasted_iota(jnp.int32, (BF16_SUBLANES, width), 0)
    return jnp.where(row == 0, 1.0, 0.0).astype(BF16)


def _silu(x):
    return x * (1.0 / (1.0 + jnp.exp(-x)))


def _mla_prep_kernel(x_ref, gpre_ref, w1_ref, gq_ref, wqa_ref, wqb_ref, gkv_ref, wkk_ref, wkv_ref,
                     cos_ref, sin_ref, qt_ref, k_ref, vt_ref, sg_ref):
    tm = x_ref.shape[1]
    h = _rms(x_ref[0], gpre_ref[...]).astype(BF16)
    proj = _dot(h, w1_ref[...])
    o = 0
    q_lat = proj[:, o:o + MLA_Q_LORA]; o += MLA_Q_LORA
    kv_lat = proj[:, o:o + MLA_KV_LORA]; o += MLA_KV_LORA
    kr_a = proj[:, o:o + LANES]; o += LANES
    kr_b = proj[:, o:o + LANES]; o += LANES
    gate = proj[:, o:o + MLA_WIDTH]
    sg_ref[0] = _silu(gate).astype(BF16)

    cos = cos_ref[...]
    sin = sin_ref[...]
    scale = LOG2E / math.sqrt(MLA_NOPE + MLA_ROPE)
    qn = _rms(q_lat, gq_ref[...]).astype(BF16)
    qa = _dot(qn, wqa_ref[...])
    qb = _dot(qn, wqb_ref[...])
    kvn = _rms(kv_lat, gkv_ref[...]).astype(BF16)
    kk = _dot(kvn, wkk_ref[...])
    vv = _dot(kvn, wkv_ref[...])
    k_rot = kr_a * cos + kr_b * sin
    ones_blk = _ones_row_block(tm)
    for hd in range(MLA_HEADS):
        sl = slice(hd * LANES, (hd + 1) * LANES)
        qh = (qa[:, sl] * cos + qb[:, sl] * sin) * scale
        qt_ref[0, hd] = qh.T.astype(BF16)
        k_ref[0, hd] = (kk[:, sl] + k_rot).astype(BF16)
        vt_ref[0, hd, MLA_V:MLA_V + BF16_SUBLANES, :] = ones_blk
    for pr in range(MLA_HEADS // 2):
        vt = vv[:, pr * LANES:(pr + 1) * LANES].T.astype(BF16)
        vt_ref[0, 2 * pr, 0:MLA_V, :] = vt[0:MLA_V]
        vt_ref[0, 2 * pr + 1, 0:MLA_V, :] = vt[MLA_V:2 * MLA_V]


def _gqa_prep_kernel(x_ref, gpre_ref, w_ref, gq_ref, gk_ref, cos_ref, sin_ref,
                     qt_ref, k_ref, vt_ref, sg_ref):
    tm = x_ref.shape[1]
    h = _rms(x_ref[0], gpre_ref[...]).astype(BF16)
    proj = _dot(h, w_ref[...])
    cos = cos_ref[...]
    sin = sin_ref[...]
    scale = LOG2E / math.sqrt(GQA_HEAD_DIM)

    def rot(xh, g):
        xn = _rms(xh, g)
        return xn * cos + pltpu.roll(xn, GQA_HEAD_DIM // 2, axis=1) * sin

    for hd in range(GQA_HEADS):
        qh = rot(proj[:, hd * LANES:(hd + 1) * LANES], gq_ref[...]) * scale
        qt_ref[0, hd] = qh.T.astype(BF16)
    ones_blk = _ones_row_block(tm)
    for g in range(GQA_KV_HEADS):
        ko = GQA_WIDTH + g * LANES
        vo = GQA_WIDTH + GQA_KV_WIDTH + g * LANES
        k_ref[0, g] = rot(proj[:, ko:ko + LANES], gk_ref[...]).astype(BF16)
        vt_ref[0, g, 0:GQA_HEAD_DIM, :] = proj[:, vo:vo + LANES].T.astype(BF16)
        vt_ref[0, g, GQA_HEAD_DIM:GQA_HEAD_DIM + BF16_SUBLANES, :] = ones_blk
    go = GQA_WIDTH + 2 * GQA_KV_WIDTH
    sg_ref[0] = _silu(proj[:, go:go + GQA_WIDTH]).astype(BF16)


def _diff_prep_kernel(x_ref, gpre_ref, w_ref, qt_ref, k_ref, vt_ref, sg_ref):
    tm = x_ref.shape[1]
    h = _rms(x_ref[0], gpre_ref[...]).astype(BF16)
    proj = _dot(h, w_ref[...])
    scale = LOG2E / math.sqrt(DIFF_HEAD_DIM)
    row = lax.broadcasted_iota(jnp.int32, (LANES, tm), 0)
    ones_blk = _ones_row_block(tm)
    for hd in range(DIFF_HEADS):
        sl = slice(hd * LANES, (hd + 1) * LANES)
        qt = (proj[:, sl] * scale).T
        qt_ref[0, hd, 0] = jnp.where(row < DIFF_HEAD_DIM, qt, 0.0).astype(BF16)
        qt_ref[0, hd, 1] = jnp.where(row >= DIFF_HEAD_DIM, qt, 0.0).astype(BF16)
        k_ref[0, hd] = proj[:, DIFF_WIDTH + hd * LANES:DIFF_WIDTH + (hd + 1) * LANES].astype(BF16)
        vo = 2 * DIFF_WIDTH + hd * LANES
        vt_ref[0, hd, 0:LANES, :] = proj[:, vo:vo + LANES].T.astype(BF16)
        vt_ref[0, hd, LANES:LANES + BF16_SUBLANES, :] = ones_blk
    sg_ref[0] = _silu(proj[:, 3 * DIFF_WIDTH:4 * DIFF_WIDTH]).astype(BF16)


def _full(shape):
    return pl.BlockSpec(shape, lambda b, t: (0,) * len(shape))


def _prep_call(kernel, name, x, consts, tables, n_heads, n_kv, qt_inner, dv, sg_width):
    bsz, seq, _ = x.shape
    tm = TOKEN_TILE
    in_specs = [pl.BlockSpec((1, tm, D_MODEL), lambda b, t: (b, t, 0))]
    in_specs += [_full(c.shape) for c in consts]
    in_specs += [pl.BlockSpec((tm, LANES), lambda b, t: (t, 0)) for _ in tables]
    qt_shape = (bsz, n_heads) + qt_inner + (LANES, seq)
    qt_block = (1, n_heads) + qt_inner + (LANES, tm)
    nq = len(qt_shape)
    out_shape = (
        jax.ShapeDtypeStruct(qt_shape, BF16),
        jax.ShapeDtypeStruct((bsz, n_kv, seq, LANES), BF16),
        jax.ShapeDtypeStruct((bsz, n_kv, dv + BF16_SUBLANES, seq), BF16),
        jax.ShapeDtypeStruct((bsz, seq, sg_width), BF16),
    )
    out_specs = (
        pl.BlockSpec(qt_block, lambda b, t: (b,) + (0,) * (nq - 2) + (t,)),
        pl.BlockSpec((1, n_kv, tm, LANES), lambda b, t: (b, 0, t, 0)),
        pl.BlockSpec((1, n_kv, dv + BF16_SUBLANES, tm), lambda b, t: (b, 0, 0, t)),
        pl.BlockSpec((1, tm, sg_width), lambda b, t: (b, t, 0)),
    )
    return pl.pallas_call(
        kernel, grid=(bsz, seq // tm), in_specs=in_specs, out_specs=out_specs, out_shape=out_shape,
        compiler_params=_cparams(("parallel", "parallel")), name=name,
    )(x, *consts, *tables)


def _flash_loop(scores, values, finalize, n_streams, n_q, n_k, stream_major,
                sa_ref, sb_ref, tmax_ref, m_ref, acc_ref):
    unroll = min(KEY_TILES_PER_TRIP, n_k)
    static_q = n_q * n_k <= MAX_STATIC_PAIRS
    assert unroll % 2 == 0 and n_k % unroll == 0
    m_ref[...] = jnp.full(m_ref.shape, -jnp.inf, F32)
    acc_ref[...] = jnp.zeros(acc_ref.shape, F32)
    bufs = (sa_ref, sb_ref)
    trips = n_k // unroll

    def fill(st, qi, j, slot):
        s = scores(st, qi, j)
        bufs[slot][st] = s
        tmax_ref[slot, st] = jnp.max(s, axis=0, keepdims=True)

    def consume(st, j, slot):
        m_prev = m_ref[st]
        m_new = jnp.maximum(m_prev, tmax_ref[slot, st])
        p = jnp.exp2(bufs[slot][st] - m_new).astype(BF16)
        alpha = jnp.exp2(m_prev - m_new)
        acc_ref[st] = acc_ref[st] * alpha + _dot(values(st, j), p)
        m_ref[st] = m_new

    def trip(qi, i, last):
        for u in range(unroll):
            j = unroll * i + u
            groups = [(st,) for st in range(n_streams)] if stream_major else [tuple(range(n_streams))]
            for group in groups:
                for st in group:
                    if not (last and u == unroll - 1):
                        fill(st, qi, j + 1, (u + 1) % 2)
                    elif not isinstance(qi, int):
                        fill(st, jnp.minimum(qi + 1, n_q - 1), 0, 0)
                    elif qi + 1 < n_q:
                        fill(st, qi + 1, 0, 0)
                for st in group:
                    consume(st, j, u % 2)

    def query_tile(qi):
        def body(i, carry):
            trip(qi, i, False)
            return carry

        lax.fori_loop(0, trips - 1, body, 0)
        trip(qi, trips - 1, True)
        finalize(qi)
        m_ref[...] = jnp.full(m_ref.shape, -jnp.inf, F32)

    for st in range(n_streams):
        fill(st, 0, 0, 0)
    if static_q:
        for qi in range(n_q):
            query_tile(qi)
    else:
        def q_body(qi, carry):
            query_tile(qi)
            return carry

        lax.fori_loop(0, n_q, q_body, 0)


def _tile_start(j, size):
    return j * size if isinstance(j, int) else pl.multiple_of(j * size, size)


def _attn_kernel(qt_ref, k_ref, vt_ref, o_ref, sa_ref, sb_ref, tmax_ref, m_ref, acc_ref,
                 *, heads, dv, tq, tk):
    shared_kv = k_ref.shape[1] == 1

    def scores(hd, qi, j):
        kv = 0 if shared_kv else hd
        return _dot(k_ref[0, kv, pl.ds(_tile_start(j, tk), tk), :],
                    qt_ref[0, hd, :, pl.ds(_tile_start(qi, tq), tq)])

    def values(hd, j):
        kv = 0 if shared_kv else hd
        return vt_ref[0, kv, :, pl.ds(_tile_start(j, tk), tk)]

    def finalize(qi):
        outs = []
        for hd in range(heads):
            acc = acc_ref[hd]
            outs.append(acc[0:dv] * (1.0 / acc[dv:dv + 1]))
        o_t = outs[0] if heads == 1 else jnp.concatenate(outs, axis=0)
        for c in range(heads * dv // LANES):
            o_ref[0, c, pl.ds(_tile_start(qi, tq), tq), :] = o_t[c * LANES:(c + 1) * LANES].T.astype(BF16)

    _flash_loop(scores, values, finalize, heads, qt_ref.shape[3] // tq, k_ref.shape[2] // tk, True,
                sa_ref, sb_ref, tmax_ref, m_ref, acc_ref)


def _attention(qt, k, vt, *, heads_per_step, dv, name):
    bsz, n_heads, _, seq = qt.shape
    n_kv = k.shape[1]
    tq = TOKEN_TILE
    tk = KEY_TILE
    tqs = tq * min(Q_TILES_PER_STEP, seq // tq)
    hps = heads_per_step
    rep = n_heads // n_kv
    kv_per_step = 1 if rep > 1 else hps
    assert (rep == 1) or (rep % hps == 0)
    kv_index = (lambda h: (h * hps) // rep) if rep > 1 else (lambda h: h)
    cols = hps * dv // LANES
    dvp = dv + BF16_SUBLANES
    kernel = functools.partial(_attn_kernel, heads=hps, dv=dv, tq=tq, tk=tk)
    return pl.pallas_call(
        kernel,
        grid=(bsz, n_heads // hps, seq // tqs),
        in_specs=[
            pl.BlockSpec((1, hps, LANES, tqs), lambda b, h, q: (b, h, 0, q)),
            pl.BlockSpec((1, kv_per_step, seq, LANES), lambda b, h, q: (b, kv_index(h), 0, 0),
                         pipeline_mode=SINGLE_BUFFER),
            pl.BlockSpec((1, kv_per_step, dvp, seq), lambda b, h, q: (b, kv_index(h), 0, 0),
                         pipeline_mode=SINGLE_BUFFER),
        ],
        out_specs=pl.BlockSpec((1, cols, tqs, LANES), lambda b, h, q: (b, h, q, 0)),
        out_shape=jax.ShapeDtypeStruct((bsz, n_heads * dv // LANES, seq, LANES), BF16),
        scratch_shapes=[pltpu.VMEM((hps, tk, tq), F32), pltpu.VMEM((hps, tk, tq), F32),
                        pltpu.VMEM((2, hps, 1, tq), F32),
                        pltpu.VMEM((hps, 1, tq), F32), pltpu.VMEM((hps, dvp, tq), F32)],
        compiler_params=_cparams(("parallel", "parallel", "arbitrary")), name=name,
    )(qt, k, vt)


def _diff_attn_kernel(lam_ref, gsub_ref, qt_ref, k_ref, vt_ref, bias_ref, o_ref,
                      sa_ref, sb_ref, tmax_ref, m_ref, acc_ref, *, tq, tk, lam_init):
    n_q = qt_ref.shape[4] // tq
    q_base = pl.program_id(2) * n_q
    e_lo, e_hi = _bias_offsets(tk, tq)

    def scores(c, qi, j):
        bias = bias_ref[0, jnp.clip(j * (tk // tq) - (q_base + qi), e_lo, e_hi) - e_lo]
        return _dot(k_ref[0, 0, pl.ds(_tile_start(j, tk), tk), :],
                    qt_ref[0, 0, c, :, pl.ds(_tile_start(qi, tq), tq)]) + bias

    def values(c, j):
        return vt_ref[0, 0, :, pl.ds(_tile_start(j, tk), tk)]

    def finalize(qi):
        lq1, lk1, lq2, lk2 = (lam_ref[i:i + 1, :] for i in range(4))
        lam = (jnp.exp(jnp.sum(lq1 * lk1, axis=1, keepdims=True))
               - jnp.exp(jnp.sum(lq2 * lk2, axis=1, keepdims=True)) + lam_init)
        a1 = acc_ref[0]
        a2 = acc_ref[1]
        o_t = (a1[0:LANES] * (1.0 / a1[LANES:LANES + 1])
               - lam * (a2[0:LANES] * (1.0 / a2[LANES:LANES + 1])))
        ms = jnp.mean(o_t * o_t, axis=0, keepdims=True)
        o_t = o_t * lax.rsqrt(ms + EPS) * gsub_ref[...] * (1.0 - lam_init)
        o_ref[0, 0, pl.ds(_tile_start(qi, tq), tq), :] = o_t.T.astype(BF16)

    _flash_loop(scores, values, finalize, 2, n_q, k_ref.shape[2] // tk, False,
                sa_ref, sb_ref, tmax_ref, m_ref, acc_ref)


def _biased_key_tile(seq):
    return KEY_TILE if seq // KEY_TILE >= KEY_TILES_PER_TRIP else TOKEN_TILE


def _diff_attention(qt, k, vt, bias, lam_vecs, g_sub_col, *, lam_init, name):
    bsz, n_heads, _, _, seq = qt.shape
    tq = TOKEN_TILE
    tk = _biased_key_tile(seq)
    assert bias.shape[2:] == (tk, tq) and tk % tq == 0
    tqs = tq * min(Q_TILES_PER_STEP, seq // tq)
    dvp = LANES + BF16_SUBLANES
    kernel = functools.partial(_diff_attn_kernel, tq=tq, tk=tk, lam_init=lam_init)
    return pl.pallas_call(
        kernel,
        grid=(bsz, n_heads, seq // tqs),
        in_specs=[
            pl.BlockSpec((4, DIFF_HEAD_DIM), lambda b, h, q: (0, 0)),
            pl.BlockSpec((LANES, 1), lambda b, h, q: (0, 0)),
            pl.BlockSpec((1, 1, 2, LANES, tqs), lambda b, h, q: (b, h, 0, 0, q)),
            pl.BlockSpec((1, 1, seq, LANES), lambda b, h, q: (b, h, 0, 0), pipeline_mode=SINGLE_BUFFER),
            pl.BlockSpec((1, 1, dvp, seq), lambda b, h, q: (b, h, 0, 0), pipeline_mode=SINGLE_BUFFER),
            pl.BlockSpec((1,) + bias.shape[1:], lambda b, h, q: (h, 0, 0, 0), pipeline_mode=SINGLE_BUFFER),
        ],
        out_specs=pl.BlockSpec((1, 1, tqs, LANES), lambda b, h, q: (b, h, q, 0)),
        out_shape=jax.ShapeDtypeStruct((bsz, n_heads, seq, LANES), BF16),
        scratch_shapes=[pltpu.VMEM((2, tk, tq), F32), pltpu.VMEM((2, tk, tq), F32),
                        pltpu.VMEM((2, 2, 1, tq), F32),
                        pltpu.VMEM((2, 1, tq), F32), pltpu.VMEM((2, dvp, tq), F32)],
        compiler_params=_cparams(("parallel", "parallel", "arbitrary")), name=name,
    )(lam_vecs, g_sub_col, qt, k, vt, bias)


def _bias_kernel(rel_ref, bucket_ref, o_ref):
    hd = pl.program_id(0)
    chunk = 64

    def body(c, carry):
        rows = pl.ds(pl.multiple_of(c * chunk, chunk), chunk)
        idx = bucket_ref[0, rows, :]
        acc = jnp.zeros(idx.shape, F32)
        for b in range(REL_BUCKETS):
            acc = jnp.where(idx == b, rel_ref[b, hd] * LOG2E, acc)
        o_ref[0, 0, rows, :] = acc
        return carry

    lax.fori_loop(0, bucket_ref.shape[1] // chunk, body, 0)


def _t5_bucket(rel):
    half = REL_BUCKETS // 2
    max_exact = half // 2
    base = (rel > 0).astype(jnp.int32) * half
    n = jnp.abs(rel)
    nf = jnp.maximum(n, 1).astype(F32)
    large = max_exact + (jnp.log(nf / max_exact) / math.log(REL_MAX_DIST / max_exact)
                         * (half - max_exact)).astype(jnp.int32)
    large = jnp.minimum(large, half - 1)
    return base + jnp.where(n < max_exact, n, large)


def _bias_offsets(tk, tq):
    lo = (-REL_MAX_DIST - tk + 1) // tq
    hi = -((-REL_MAX_DIST - tq + 1) // tq)
    return lo, hi


def _bias_tiles(rel_bias, tk, tq):
    lo, hi = _bias_offsets(tk, tq)
    kk = jnp.arange(tk)[:, None]
    qq = jnp.arange(tq)[None, :]
    buckets = jnp.stack([_t5_bucket(e * tq + kk - qq) for e in range(lo, hi + 1)])
    n = hi - lo + 1
    return pl.pallas_call(
        _bias_kernel,
        grid=(DIFF_HEADS, n),
        in_specs=[pl.BlockSpec(memory_space=pltpu.SMEM),
                  pl.BlockSpec((1, tk, tq), lambda h, e: (e, 0, 0))],
        out_specs=pl.BlockSpec((1, 1, tk, tq), lambda h, e: (h, e, 0, 0)),
        out_shape=jax.ShapeDtypeStruct((DIFF_HEADS, n, tk, tq), F32),
        compiler_params=_cparams(("parallel", "parallel")), name=f"t5_bias_tiles_{tk}",
    )(rel_bias, buckets)


def _out_kernel(x_ref, o_ref, sg_ref, w_ref, g_ref, y_ref):
    cols = o_ref.shape[1]
    gated = [(o_ref[0, c].astype(F32) * sg_ref[0, :, c * LANES:(c + 1) * LANES].astype(F32)).astype(BF16)
             for c in range(cols)]
    m = _dot(jnp.concatenate(gated, axis=1), w_ref[...])
    y_ref[0] = x_ref[0] + _rms(m, g_ref[...])


def _out_call(x, o, sg, w_o, g_post, name):
    bsz, seq, _ = x.shape
    tm = TOKEN_TILE
    cols = o.shape[1]
    return pl.pallas_call(
        _out_kernel,
        grid=(bsz, seq // tm),
        in_specs=[
            pl.BlockSpec((1, tm, D_MODEL), lambda b, t: (b, t, 0)),
            pl.BlockSpec((1, cols, tm, LANES), lambda b, t: (b, 0, t, 0)),
            pl.BlockSpec((1, tm, cols * LANES), lambda b, t: (b, t, 0)),
            _full(w_o.shape),
            _full(g_post.shape),
        ],
        out_specs=pl.BlockSpec((1, tm, D_MODEL), lambda b, t: (b, t, 0)),
        out_shape=jax.ShapeDtypeStruct(x.shape, F32),
        compiler_params=_cparams(("parallel", "parallel")), name=name,
    )(x, o, sg, w_o, g_post)


def _rope_angles(pos, dim):
    inv = ROPE_THETA ** (-(jnp.arange(0, dim, 2, dtype=F32) / dim))
    return pos.astype(F32)[:, None] * inv[None, :]


def _mla_tables(seq):
    ang = _rope_angles(jnp.arange(seq), MLA_ROPE)
    cos, sin = jnp.cos(ang), jnp.sin(ang)
    pad = LANES - MLA_NOPE - MLA_ROPE
    cos_t = jnp.concatenate([jnp.ones((seq, MLA_NOPE), F32), cos, cos, jnp.zeros((seq, pad), F32)], axis=1)
    sin_t = jnp.concatenate([jnp.zeros((seq, MLA_NOPE), F32), -sin, sin, jnp.zeros((seq, pad), F32)], axis=1)
    return cos_t, sin_t


def _gqa_tables(seq):
    rows = seq // GRID_W
    row = jnp.repeat(jnp.arange(rows), GRID_W)
    col = jnp.tile(jnp.arange(GRID_W), rows)
    ang = jnp.concatenate([_rope_angles(row, GQA_HEAD_DIM // 2), _rope_angles(col, GQA_HEAD_DIM // 2)], axis=-1)
    cos, sin = jnp.cos(ang), jnp.sin(ang)
    return jnp.concatenate([cos, cos], axis=1), jnp.concatenate([-sin, sin], axis=1)


def _mla_weights(w_in, w_uq, w_ukv):
    half = MLA_ROPE // 2
    pad = LANES - MLA_NOPE - MLA_ROPE
    o_kv = MLA_Q_LORA
    o_kr = o_kv + MLA_KV_LORA
    o_gate = o_kr + MLA_ROPE
    w_kr = w_in[:, o_kr:o_gate]
    z = lambda n, rows: jnp.zeros((rows, n), F32)
    kr_a = jnp.concatenate([z(MLA_NOPE, D_MODEL), w_kr, z(pad, D_MODEL)], axis=1)
    kr_b = jnp.concatenate([z(MLA_NOPE, D_MODEL), w_kr[:, half:], w_kr[:, :half], z(pad, D_MODEL)], axis=1)
    w1 = jnp.concatenate([w_in[:, :o_kr], kr_a, kr_b, w_in[:, o_gate:]], axis=1)
    uq = w_uq.reshape(MLA_Q_LORA, MLA_HEADS, MLA_NOPE + MLA_ROPE)
    nope, r1, r2 = uq[..., :MLA_NOPE], uq[..., MLA_NOPE:MLA_NOPE + half], uq[..., MLA_NOPE + half:]
    zq = jnp.zeros((MLA_Q_LORA, MLA_HEADS, pad), F32)
    wq_a = jnp.concatenate([nope, r1, r2, zq], axis=-1).reshape(MLA_Q_LORA, MLA_HEADS * LANES)
    wq_b = jnp.concatenate([jnp.zeros_like(nope), r2, r1, zq], axis=-1).reshape(MLA_Q_LORA, MLA_HEADS * LANES)
    ukv = w_ukv.reshape(MLA_KV_LORA, MLA_HEADS, MLA_NOPE + MLA_V)
    wk = jnp.concatenate([ukv[..., :MLA_NOPE], jnp.zeros((MLA_KV_LORA, MLA_HEADS, LANES - MLA_NOPE), F32)],
                         axis=-1).reshape(MLA_KV_LORA, MLA_HEADS * LANES)
    wv = ukv[..., MLA_NOPE:].reshape(MLA_KV_LORA, MLA_WIDTH)
    return tuple(w.astype(BF16) for w in (w1, wq_a, wq_b, wk, wv))


def _trunk(x, tag, norm_pre, norm_post, bias_tiles, mla, gqa, dif):
    bsz, seq, _ = x.shape
    ia = ib = ic = 0
    for layer in range(DEPTH):
        kind = layer % N_MIXERS
        g_pre = norm_pre[layer][None, :]
        g_post = norm_post[layer][None, :]
        nm = f"{tag}_l{layer}"
        if kind == 0:
            w1, wq_a, wq_b, wk, wv, g_q, g_kv, w_o = mla[ia]
            ia += 1
            qt, k, vt, sg = _prep_call(
                _mla_prep_kernel, nm + "_mla_prep", x,
                (g_pre, w1, g_q[None, :], wq_a, wq_b, g_kv[None, :], wk, wv), _mla_tables(seq),
                MLA_HEADS, MLA_HEADS, (), MLA_V, MLA_WIDTH)
            o = _attention(qt, k, vt, heads_per_step=2, dv=MLA_V, name=nm + "_mla_attn")
        elif kind == 1:
            w, g_q, g_k, w_o = gqa[ib]
            ib += 1
            qt, k, vt, sg = _prep_call(
                _gqa_prep_kernel, nm + "_gqa_prep", x, (g_pre, w, g_q[None, :], g_k[None, :]),
                _gqa_tables(seq), GQA_HEADS, GQA_KV_HEADS, (), GQA_HEAD_DIM, GQA_WIDTH)
            o = _attention(qt, k, vt, heads_per_step=2, dv=GQA_HEAD_DIM, name=nm + "_gqa_attn")
        else:
            w, lam_vecs, g_sub, w_o = dif[ic]
            ic += 1
            lam_init = 0.8 - 0.6 * math.exp(-0.3 * layer)
            qt, k, vt, sg = _prep_call(
                _diff_prep_kernel, nm + "_dif_prep", x, (g_pre, w), (),
                DIFF_HEADS, DIFF_HEADS, (2,), 2 * DIFF_HEAD_DIM, DIFF_WIDTH)
            o = _diff_attention(qt, k, vt, bias_tiles[_biased_key_tile(seq)], lam_vecs, g_sub[:, None],
                                lam_init=lam_init, name=nm + "_dif_attn")
        x = _out_call(x, o, sg, w_o, g_post, nm + "_out")
    return x


def kernel(x_prompt, x_sample, norm_pre, norm_post, rel_bias, mla_w_in, mla_g_q, mla_w_uq, mla_g_kv, mla_w_ukv, mla_w_o, gqa_w_in, gqa_g_q, gqa_g_k, gqa_w_o, dif_w_in, dif_lam_q1, dif_lam_k1, dif_lam_q2, dif_lam_k2, dif_g_sub, dif_w_o):
    mla = []
    for i in range(mla_w_in.shape[0]):
        mla.append(_mla_weights(mla_w_in[i], mla_w_uq[i], mla_w_ukv[i])
                   + (mla_g_q[i], mla_g_kv[i], mla_w_o[i].astype(BF16)))
    gqa = [(gqa_w_in[i].astype(BF16), gqa_g_q[i], gqa_g_k[i], gqa_w_o[i].astype(BF16))
           for i in range(gqa_w_in.shape[0])]
    dif = [(dif_w_in[i].astype(BF16),
            jnp.stack([dif_lam_q1[i], dif_lam_k1[i], dif_lam_q2[i], dif_lam_k2[i]]),
            dif_g_sub[i], dif_w_o[i].astype(BF16))
           for i in range(dif_w_in.shape[0])]
    bias_tiles = {tk: _bias_tiles(rel_bias, tk, TOKEN_TILE)
                  for tk in sorted({_biased_key_tile(x.shape[1]) for x in (x_prompt, x_sample)})}
    y_prompt = _trunk(x_prompt, "p", norm_pre, norm_post, bias_tiles, mla, gqa, dif)
    y_sample = _trunk(x_sample, "s", norm_pre, norm_post, bias_tiles, mla, gqa, dif)
    return (y_prompt, y_sample)
```

```python
import functools
import math

import jax
import jax.numpy as jnp
from jax import lax
from jax.experimental import pallas as pl
from jax.experimental.pallas import tpu as pltpu

D_MODEL = 1024
DEPTH = 4
N_MIXERS = 3
EPS = 1e-6
ROPE_THETA = 10000.0
GRID_W = 64

MLA_HEADS = 16
MLA_Q_LORA = 256
MLA_KV_LORA = 128
MLA_NOPE = 64
MLA_ROPE = 32
MLA_V = 64
MLA_WIDTH = MLA_HEADS * MLA_V

GQA_HEADS = 8
GQA_KV_HEADS = 2
GQA_HEAD_DIM = 128
GQA_REP = GQA_HEADS // GQA_KV_HEADS
GQA_WIDTH = GQA_HEADS * GQA_HEAD_DIM
GQA_KV_WIDTH = GQA_KV_HEADS * GQA_HEAD_DIM

DIFF_HEADS = 8
DIFF_HEAD_DIM = 64
DIFF_WIDTH = DIFF_HEADS * 2 * DIFF_HEAD_DIM

REL_BUCKETS = 32
REL_MAX_DIST = 128

LANES = 128
BF16_SUBLANES = 16
TOKEN_TILE = 512
KEY_TILE = 1024
Q_TILES_PER_STEP = 8
KEY_TILES_PER_TRIP = 4
MAX_STATIC_PAIRS = 8
VMEM_LIMIT = 56 * 1024 * 1024
LOG2E = math.log2(math.e)

def _head_block_mode(seq, tokens_per_step):
    return pl.Buffered(1) if seq // tokens_per_step > 1 else None

F32 = jnp.float32
BF16 = jnp.bfloat16


def _cparams(semantics):
    return pltpu.CompilerParams(dimension_semantics=semantics, vmem_limit_bytes=VMEM_LIMIT)


def _rms(x, g):
    return x * lax.rsqrt(jnp.mean(x * x, axis=-1, keepdims=True) + EPS) * g


def _dot(a, b):
    return jnp.dot(a, b, preferred_element_type=F32)


def _ones_row_block(width):
    row = lax.broadcasted_iota(jnp.int32, (BF16_SUBLANES, width), 0)
    return jnp.where(row == 0, 1.0, 0.0).astype(BF16)


def _silu(x):
    return x * (1.0 / (1.0 + jnp.exp(-x)))


def _mla_prep_kernel(x_ref, gpre_ref, w1_ref, gq_ref, wqa_ref, wqb_ref, gkv_ref, wkk_ref, wkv_ref,
                     cos_ref, sin_ref, qt_ref, k_ref, vt_ref, sg_ref):
    tm = x_ref.shape[1]
    h = _rms(x_ref[0], gpre_ref[...]).astype(BF16)
    proj = _dot(h, w1_ref[...])
    o = 0
    q_lat = proj[:, o:o + MLA_Q_LORA]; o += MLA_Q_LORA
    kv_lat = proj[:, o:o + MLA_KV_LORA]; o += MLA_KV_LORA
    kr_a = proj[:, o:o + LANES]; o += LANES
    kr_b = proj[:, o:o + LANES]; o += LANES
    gate = proj[:, o:o + MLA_WIDTH]
    sg_ref[0] = _silu(gate).astype(BF16)

    cos = cos_ref[...]
    sin = sin_ref[...]
    scale = LOG2E / math.sqrt(MLA_NOPE + MLA_ROPE)
    qn = _rms(q_lat, gq_ref[...]).astype(BF16)
    qa = _dot(qn, wqa_ref[...])
    qb = _dot(qn, wqb_ref[...])
    kvn = _rms(kv_lat, gkv_ref[...]).astype(BF16)
    kk = _dot(kvn, wkk_ref[...])
    vv = _dot(kvn, wkv_ref[...])
    k_rot = kr_a * cos + kr_b * sin
    ones_blk = _ones_row_block(tm)
    for hd in range(MLA_HEADS):
        sl = slice(hd * LANES, (hd + 1) * LANES)
        qh = (qa[:, sl] * cos + qb[:, sl] * sin) * scale
        qt_ref[0, hd] = qh.T.astype(BF16)
        k_ref[0, hd] = (kk[:, sl] + k_rot).astype(BF16)
        vt_ref[0, hd, MLA_V:MLA_V + BF16_SUBLANES, :] = ones_blk
    for pr in range(MLA_HEADS // 2):
        vt = vv[:, pr * LANES:(pr + 1) * LANES].T.astype(BF16)
        vt_ref[0, 2 * pr, 0:MLA_V, :] = vt[0:MLA_V]
        vt_ref[0, 2 * pr + 1, 0:MLA_V, :] = vt[MLA_V:2 * MLA_V]


def _gqa_prep_kernel(x_ref, gpre_ref, w_ref, gq_ref, gk_ref, cos_ref, sin_ref,
                     qt_ref, k_ref, vt_ref, sg_ref):
    tm = x_ref.shape[1]
    h = _rms(x_ref[0], gpre_ref[...]).astype(BF16)
    proj = _dot(h, w_ref[...])
    cos = cos_ref[...]
    sin = sin_ref[...]
    scale = LOG2E / math.sqrt(GQA_HEAD_DIM)

    def rot(xh, g):
        xn = _rms(xh, g)
        return xn * cos + pltpu.roll(xn, GQA_HEAD_DIM // 2, axis=1) * sin

    for hd in range(GQA_HEADS):
        qh = rot(proj[:, hd * LANES:(hd + 1) * LANES], gq_ref[...]) * scale
        qt_ref[0, hd] = qh.T.astype(BF16)
    ones_blk = _ones_row_block(tm)
    for g in range(GQA_KV_HEADS):
        ko = GQA_WIDTH + g * LANES
        vo = GQA_WIDTH + GQA_KV_WIDTH + g * LANES
        k_ref[0, g] = rot(proj[:, ko:ko + LANES], gk_ref[...]).astype(BF16)
        vt_ref[0, g, 0:GQA_HEAD_DIM, :] = proj[:, vo:vo + LANES].T.astype(BF16)
        vt_ref[0, g, GQA_HEAD_DIM:GQA_HEAD_DIM + BF16_SUBLANES, :] = ones_blk
    go = GQA_WIDTH + 2 * GQA_KV_WIDTH
    sg_ref[0] = _silu(proj[:, go:go + GQA_WIDTH]).astype(BF16)


def _diff_prep_kernel(x_ref, gpre_ref, w_ref, qt_ref, k_ref, vt_ref, sg_ref):
    tm = x_ref.shape[1]
    h = _rms(x_ref[0], gpre_ref[...]).astype(BF16)
    proj = _dot(h, w_ref[...])
    scale = LOG2E / math.sqrt(DIFF_HEAD_DIM)
    row = lax.broadcasted_iota(jnp.int32, (LANES, tm), 0)
    ones_blk = _ones_row_block(tm)
    for hd in range(DIFF_HEADS):
        sl = slice(hd * LANES, (hd + 1) * LANES)
        qt = (proj[:, sl] * scale).T
        qt_ref[0, hd, 0] = jnp.where(row < DIFF_HEAD_DIM, qt, 0.0).astype(BF16)
        qt_ref[0, hd, 1] = jnp.where(row >= DIFF_HEAD_DIM, qt, 0.0).astype(BF16)
        k_ref[0, hd] = proj[:, DIFF_WIDTH + hd * LANES:DIFF_WIDTH + (hd + 1) * LANES].astype(BF16)
        vo = 2 * DIFF_WIDTH + hd * LANES
        vt_ref[0, hd, 0:LANES, :] = proj[:, vo:vo + LANES].T.astype(BF16)
        vt_ref[0, hd, LANES:LANES + BF16_SUBLANES, :] = ones_blk
    sg_ref[0] = _silu(proj[:, 3 * DIFF_WIDTH:4 * DIFF_WIDTH]).astype(BF16)


def _full(shape):
    return pl.BlockSpec(shape, lambda b, t: (0,) * len(shape))


def _prep_call(kernel, name, x, consts, tables, n_heads, n_kv, qt_inner, dv, sg_width):
    bsz, seq, _ = x.shape
    tm = TOKEN_TILE
    in_specs = [pl.BlockSpec((1, tm, D_MODEL), lambda b, t: (b, t, 0))]
    in_specs += [_full(c.shape) for c in consts]
    in_specs += [pl.BlockSpec((tm, LANES), lambda b, t: (t, 0)) for _ in tables]
    qt_shape = (bsz, n_heads) + qt_inner + (LANES, seq)
    qt_block = (1, n_heads) + qt_inner + (LANES, tm)
    nq = len(qt_shape)
    out_shape = (
        jax.ShapeDtypeStruct(qt_shape, BF16),
        jax.ShapeDtypeStruct((bsz, n_kv, seq, LANES), BF16),
        jax.ShapeDtypeStruct((bsz, n_kv, dv + BF16_SUBLANES, seq), BF16),
        jax.ShapeDtypeStruct((bsz, seq, sg_width), BF16),
    )
    out_specs = (
        pl.BlockSpec(qt_block, lambda b, t: (b,) + (0,) * (nq - 2) + (t,)),
        pl.BlockSpec((1, n_kv, tm, LANES), lambda b, t: (b, 0, t, 0)),
        pl.BlockSpec((1, n_kv, dv + BF16_SUBLANES, tm), lambda b, t: (b, 0, 0, t)),
        pl.BlockSpec((1, tm, sg_width), lambda b, t: (b, t, 0)),
    )
    return pl.pallas_call(
        kernel, grid=(bsz, seq // tm), in_specs=in_specs, out_specs=out_specs, out_shape=out_shape,
        compiler_params=_cparams(("parallel", "parallel")), name=name,
    )(x, *consts, *tables)


def _flash_loop(scores, values, finalize, n_streams, n_q, n_k, sa_ref, sb_ref, tmax_ref, m_ref, acc_ref):
    unroll = min(KEY_TILES_PER_TRIP, n_k)
    static_q = n_q * n_k <= MAX_STATIC_PAIRS
    assert unroll % 2 == 0 and n_k % unroll == 0
    m_ref[...] = jnp.full(m_ref.shape, -jnp.inf, F32)
    acc_ref[...] = jnp.zeros(acc_ref.shape, F32)
    bufs = (sa_ref, sb_ref)
    trips = n_k // unroll

    def fill(st, qi, j, slot):
        s = scores(st, qi, j)
        bufs[slot][st] = s
        tmax_ref[slot, st] = jnp.max(s, axis=0, keepdims=True)

    def consume(st, j, slot):
        m_prev = m_ref[st]
        m_new = jnp.maximum(m_prev, tmax_ref[slot, st])
        p = jnp.exp2(bufs[slot][st] - m_new).astype(BF16)
        alpha = jnp.exp2(m_prev - m_new)
        acc_ref[st] = acc_ref[st] * alpha + _dot(values(st, j), p)
        m_ref[st] = m_new

    def trip(qi, i, last):
        for u in range(unroll):
            j = unroll * i + u
            for st in range(n_streams):
                if not (last and u == unroll - 1):
                    fill(st, qi, j + 1, (u + 1) % 2)
                elif not isinstance(qi, int):
                    fill(st, jnp.minimum(qi + 1, n_q - 1), 0, 0)
                elif qi + 1 < n_q:
                    fill(st, qi + 1, 0, 0)
                consume(st, j, u % 2)

    def query_tile(qi):
        def body(i, carry):
            trip(qi, i, False)
            return carry

        lax.fori_loop(0, trips - 1, body, 0)
        trip(qi, trips - 1, True)
        finalize(qi)
        m_ref[...] = jnp.full(m_ref.shape, -jnp.inf, F32)

    for st in range(n_streams):
        fill(st, 0, 0, 0)
    if static_q:
        for qi in range(n_q):
            query_tile(qi)
    else:
        def q_body(qi, carry):
            query_tile(qi)
            return carry

        lax.fori_loop(0, n_q, q_body, 0)


def _tile_start(j, size):
    return j * size if isinstance(j, int) else pl.multiple_of(j * size, size)


def _attn_kernel(qt_ref, k_ref, vt_ref, o_ref, sa_ref, sb_ref, tmax_ref, m_ref, acc_ref,
                 *, heads, dv, tq, tk):
    shared_kv = k_ref.shape[1] == 1

    def scores(hd, qi, j):
        kv = 0 if shared_kv else hd
        return _dot(k_ref[0, kv, pl.ds(_tile_start(j, tk), tk), :],
                    qt_ref[0, hd, :, pl.ds(_tile_start(qi, tq), tq)])

    def values(hd, j):
        kv = 0 if shared_kv else hd
        return vt_ref[0, kv, :, pl.ds(_tile_start(j, tk), tk)]

    def finalize(qi):
        outs = []
        for hd in range(heads):
            acc = acc_ref[hd]
            outs.append(acc[0:dv] * (1.0 / acc[dv:dv + 1]))
        o_t = outs[0] if heads == 1 else jnp.concatenate(outs, axis=0)
        for c in range(heads * dv // LANES):
            o_ref[0, c, pl.ds(_tile_start(qi, tq), tq), :] = o_t[c * LANES:(c + 1) * LANES].T.astype(BF16)

    _flash_loop(scores, values, finalize, heads, qt_ref.shape[3] // tq, k_ref.shape[2] // tk,
                sa_ref, sb_ref, tmax_ref, m_ref, acc_ref)


def _attention(qt, k, vt, *, heads_per_step, dv, name):
    bsz, n_heads, _, seq = qt.shape
    n_kv = k.shape[1]
    tq = TOKEN_TILE
    tk = KEY_TILE
    tqs = tq * min(Q_TILES_PER_STEP, seq // tq)
    hps = heads_per_step
    rep = n_heads // n_kv
    kv_per_step = 1 if rep > 1 else hps
    assert (rep == 1) or (rep % hps == 0)
    kv_index = (lambda h: (h * hps) // rep) if rep > 1 else (lambda h: h)
    cols = hps * dv // LANES
    dvp = dv + BF16_SUBLANES
    kernel = functools.partial(_attn_kernel, heads=hps, dv=dv, tq=tq, tk=tk)
    return pl.pallas_call(
        kernel,
        grid=(bsz, n_heads // hps, seq // tqs),
        in_specs=[
            pl.BlockSpec((1, hps, LANES, tqs), lambda b, h, q: (b, h, 0, q)),
            pl.BlockSpec((1, kv_per_step, seq, LANES), lambda b, h, q: (b, kv_index(h), 0, 0),
                         pipeline_mode=_head_block_mode(seq, tqs)),
            pl.BlockSpec((1, kv_per_step, dvp, seq), lambda b, h, q: (b, kv_index(h), 0, 0),
                         pipeline_mode=_head_block_mode(seq, tqs)),
        ],
        out_specs=pl.BlockSpec((1, cols, tqs, LANES), lambda b, h, q: (b, h, q, 0)),
        out_shape=jax.ShapeDtypeStruct((bsz, n_heads * dv // LANES, seq, LANES), BF16),
        scratch_shapes=[pltpu.VMEM((hps, tk, tq), F32), pltpu.VMEM((hps, tk, tq), F32),
                        pltpu.VMEM((2, hps, 1, tq), F32),
                        pltpu.VMEM((hps, 1, tq), F32), pltpu.VMEM((hps, dvp, tq), F32)],
        compiler_params=_cparams(("parallel", "parallel", "arbitrary")), name=name,
    )(qt, k, vt)


def _diff_attn_kernel(lam_ref, gsub_ref, qt_ref, k_ref, vt_ref, bias_ref, o_ref,
                      sa_ref, sb_ref, tmax_ref, m_ref, acc_ref, *, tq, tk, lam_init):
    n_q = qt_ref.shape[4] // tq
    q_base = pl.program_id(2) * n_q
    e_lo, e_hi = _bias_offsets(tk, tq)

    def scores(c, qi, j):
        bias = bias_ref[0, jnp.clip(j * (tk // tq) - (q_base + qi), e_lo, e_hi) - e_lo]
        return _dot(k_ref[0, 0, pl.ds(_tile_start(j, tk), tk), :],
                    qt_ref[0, 0, c, :, pl.ds(_tile_start(qi, tq), tq)]) + bias

    def values(c, j):
        return vt_ref[0, 0, :, pl.ds(_tile_start(j, tk), tk)]

    def finalize(qi):
        lq1, lk1, lq2, lk2 = (lam_ref[i:i + 1, :] for i in range(4))
        lam = (jnp.exp(jnp.sum(lq1 * lk1, axis=1, keepdims=True))
               - jnp.exp(jnp.sum(lq2 * lk2, axis=1, keepdims=True)) + lam_init)
        a1 = acc_ref[0]
        a2 = acc_ref[1]
        o_t = (a1[0:LANES] * (1.0 / a1[LANES:LANES + 1])
               - lam * (a2[0:LANES] * (1.0 / a2[LANES:LANES + 1])))
        ms = jnp.mean(o_t * o_t, axis=0, keepdims=True)
        o_t = o_t * lax.rsqrt(ms + EPS) * gsub_ref[...] * (1.0 - lam_init)
        o_ref[0, 0, pl.ds(_tile_start(qi, tq), tq), :] = o_t.T.astype(BF16)

    _flash_loop(scores, values, finalize, 2, n_q, k_ref.shape[2] // tk,
                sa_ref, sb_ref, tmax_ref, m_ref, acc_ref)


def _diff_attention(qt, k, vt, bias, lam_vecs, g_sub_col, *, lam_init, name):
    bsz, n_heads, _, _, seq = qt.shape
    tq = TOKEN_TILE
    tk = KEY_TILE
    assert bias.shape[2:] == (tk, tq) and tk % tq == 0
    tqs = tq * min(Q_TILES_PER_STEP, seq // tq)
    dvp = LANES + BF16_SUBLANES
    kernel = functools.partial(_diff_attn_kernel, tq=tq, tk=tk, lam_init=lam_init)
    return pl.pallas_call(
        kernel,
        grid=(bsz, n_heads, seq // tqs),
        in_specs=[
            pl.BlockSpec((4, DIFF_HEAD_DIM), lambda b, h, q: (0, 0)),
            pl.BlockSpec((LANES, 1), lambda b, h, q: (0, 0)),
            pl.BlockSpec((1, 1, 2, LANES, tqs), lambda b, h, q: (b, h, 0, 0, q)),
            pl.BlockSpec((1, 1, seq, LANES), lambda b, h, q: (b, h, 0, 0), pipeline_mode=_head_block_mode(seq, tqs)),
            pl.BlockSpec((1, 1, dvp, seq), lambda b, h, q: (b, h, 0, 0), pipeline_mode=_head_block_mode(seq, tqs)),
            pl.BlockSpec((1,) + bias.shape[1:], lambda b, h, q: (h, 0, 0, 0), pipeline_mode=_head_block_mode(seq, tqs)),
        ],
        out_specs=pl.BlockSpec((1, 1, tqs, LANES), lambda b, h, q: (b, h, q, 0)),
        out_shape=jax.ShapeDtypeStruct((bsz, n_heads, seq, LANES), BF16),
        scratch_shapes=[pltpu.VMEM((2, tk, tq), F32), pltpu.VMEM((2, tk, tq), F32),
                        pltpu.VMEM((2, 2, 1, tq), F32),
                        pltpu.VMEM((2, 1, tq), F32), pltpu.VMEM((2, dvp, tq), F32)],
        compiler_params=_cparams(("parallel", "parallel", "arbitrary")), name=name,
    )(lam_vecs, g_sub_col, qt, k, vt, bias)


def _bias_kernel(rel_ref, bucket_ref, o_ref):
    hd = pl.program_id(0)
    chunk = 64

    def body(c, carry):
        rows = pl.ds(pl.multiple_of(c * chunk, chunk), chunk)
        idx = bucket_ref[0, rows, :]
        acc = jnp.zeros(idx.shape, F32)
        for b in range(REL_BUCKETS):
            acc = jnp.where(idx == b, rel_ref[b, hd] * LOG2E, acc)
        o_ref[0, 0, rows, :] = acc
        return carry

    lax.fori_loop(0, bucket_ref.shape[1] // chunk, body, 0)


def _t5_bucket(rel):
    half = REL_BUCKETS // 2
    max_exact = half // 2
    base = (rel > 0).astype(jnp.int32) * half
    n = jnp.abs(rel)
    nf = jnp.maximum(n, 1).astype(F32)
    large = max_exact + (jnp.log(nf / max_exact) / math.log(REL_MAX_DIST / max_exact)
                         * (half - max_exact)).astype(jnp.int32)
    large = jnp.minimum(large, half - 1)
    return base + jnp.where(n < max_exact, n, large)


def _bias_offsets(tk, tq):
    lo = (-REL_MAX_DIST - tk + 1) // tq
    hi = -((-REL_MAX_DIST - tq + 1) // tq)
    return lo, hi


def _bias_tiles(rel_bias, tk, tq):
    lo, hi = _bias_offsets(tk, tq)
    kk = jnp.arange(tk)[:, None]
    qq = jnp.arange(tq)[None, :]
    buckets = jnp.stack([_t5_bucket(e * tq + kk - qq) for e in range(lo, hi + 1)])
    n = hi - lo + 1
    return pl.pallas_call(
        _bias_kernel,
        grid=(DIFF_HEADS, n),
        in_specs=[pl.BlockSpec(memory_space=pltpu.SMEM),
                  pl.BlockSpec((1, tk, tq), lambda h, e: (e, 0, 0))],
        out_specs=pl.BlockSpec((1, 1, tk, tq), lambda h, e: (h, e, 0, 0)),
        out_shape=jax.ShapeDtypeStruct((DIFF_HEADS, n, tk, tq), F32),
        compiler_params=_cparams(("parallel", "parallel")), name="t5_bias_tiles",
    )(rel_bias, buckets)


def _out_kernel(x_ref, o_ref, sg_ref, w_ref, g_ref, y_ref):
    cols = o_ref.shape[1]
    gated = [(o_ref[0, c].astype(F32) * sg_ref[0, :, c * LANES:(c + 1) * LANES].astype(F32)).astype(BF16)
             for c in range(cols)]
    m = _dot(jnp.concatenate(gated, axis=1), w_ref[...])
    y_ref[0] = x_ref[0] + _rms(m, g_ref[...])


def _out_call(x, o, sg, w_o, g_post, name):
    bsz, seq, _ = x.shape
    tm = TOKEN_TILE
    cols = o.shape[1]
    return pl.pallas_call(
        _out_kernel,
        grid=(bsz, seq // tm),
        in_specs=[
            pl.BlockSpec((1, tm, D_MODEL), lambda b, t: (b, t, 0)),
            pl.BlockSpec((1, cols, tm, LANES), lambda b, t: (b, 0, t, 0)),
            pl.BlockSpec((1, tm, cols * LANES), lambda b, t: (b, t, 0)),
            _full(w_o.shape),
            _full(g_post.shape),
        ],
        out_specs=pl.BlockSpec((1, tm, D_MODEL), lambda b, t: (b, t, 0)),
        out_shape=jax.ShapeDtypeStruct(x.shape, F32),
        compiler_params=_cparams(("parallel", "parallel")), name=name,
    )(x, o, sg, w_o, g_post)


def _rope_angles(pos, dim):
    inv = ROPE_THETA ** (-(jnp.arange(0, dim, 2, dtype=F32) / dim))
    return pos.astype(F32)[:, None] * inv[None, :]


def _mla_tables(seq):
    ang = _rope_angles(jnp.arange(seq), MLA_ROPE)
    cos, sin = jnp.cos(ang), jnp.sin(ang)
    pad = LANES - MLA_NOPE - MLA_ROPE
    cos_t = jnp.concatenate([jnp.ones((seq, MLA_NOPE), F32), cos, cos, jnp.zeros((seq, pad), F32)], axis=1)
    sin_t = jnp.concatenate([jnp.zeros((seq, MLA_NOPE), F32), -sin, sin, jnp.zeros((seq, pad), F32)], axis=1)
    return cos_t, sin_t


def _gqa_tables(seq):
    rows = seq // GRID_W
    row = jnp.repeat(jnp.arange(rows), GRID_W)
    col = jnp.tile(jnp.arange(GRID_W), rows)
    ang = jnp.concatenate([_rope_angles(row, GQA_HEAD_DIM // 2), _rope_angles(col, GQA_HEAD_DIM // 2)], axis=-1)
    cos, sin = jnp.cos(ang), jnp.sin(ang)
    return jnp.concatenate([cos, cos], axis=1), jnp.concatenate([-sin, sin], axis=1)


def _mla_weights(w_in, w_uq, w_ukv):
    half = MLA_ROPE // 2
    pad = LANES - MLA_NOPE - MLA_ROPE
    o_kv = MLA_Q_LORA
    o_kr = o_kv + MLA_KV_LORA
    o_gate = o_kr + MLA_ROPE
    w_kr = w_in[:, o_kr:o_gate]
    z = lambda n, rows: jnp.zeros((rows, n), F32)
    kr_a = jnp.concatenate([z(MLA_NOPE, D_MODEL), w_kr, z(pad, D_MODEL)], axis=1)
    kr_b = jnp.concatenate([z(MLA_NOPE, D_MODEL), w_kr[:, half:], w_kr[:, :half], z(pad, D_MODEL)], axis=1)
    w1 = jnp.concatenate([w_in[:, :o_kr], kr_a, kr_b, w_in[:, o_gate:]], axis=1)
    uq = w_uq.reshape(MLA_Q_LORA, MLA_HEADS, MLA_NOPE + MLA_ROPE)
    nope, r1, r2 = uq[..., :MLA_NOPE], uq[..., MLA_NOPE:MLA_NOPE + half], uq[..., MLA_NOPE + half:]
    zq = jnp.zeros((MLA_Q_LORA, MLA_HEADS, pad), F32)
    wq_a = jnp.concatenate([nope, r1, r2, zq], axis=-1).reshape(MLA_Q_LORA, MLA_HEADS * LANES)
    wq_b = jnp.concatenate([jnp.zeros_like(nope), r2, r1, zq], axis=-1).reshape(MLA_Q_LORA, MLA_HEADS * LANES)
    ukv = w_ukv.reshape(MLA_KV_LORA, MLA_HEADS, MLA_NOPE + MLA_V)
    wk = jnp.concatenate([ukv[..., :MLA_NOPE], jnp.zeros((MLA_KV_LORA, MLA_HEADS, LANES - MLA_NOPE), F32)],
                         axis=-1).reshape(MLA_KV_LORA, MLA_HEADS * LANES)
    wv = ukv[..., MLA_NOPE:].reshape(MLA_KV_LORA, MLA_WIDTH)
    return tuple(w.astype(BF16) for w in (w1, wq_a, wq_b, wk, wv))


def _trunk(x, tag, norm_pre, norm_post, bias_tiles, mla, gqa, dif):
    bsz, seq, _ = x.shape
    ia = ib = ic = 0
    for layer in range(DEPTH):
        kind = layer % N_MIXERS
        g_pre = norm_pre[layer][None, :]
        g_post = norm_post[layer][None, :]
        nm = f"{tag}_l{layer}"
        if kind == 0:
            w1, wq_a, wq_b, wk, wv, g_q, g_kv, w_o = mla[ia]
            ia += 1
            qt, k, vt, sg = _prep_call(
                _mla_prep_kernel, nm + "_mla_prep", x,
                (g_pre, w1, g_q[None, :], wq_a, wq_b, g_kv[None, :], wk, wv), _mla_tables(seq),
                MLA_HEADS, MLA_HEADS, (), MLA_V, MLA_WIDTH)
            o = _attention(qt, k, vt, heads_per_step=2, dv=MLA_V, name=nm + "_mla_attn")
        elif kind == 1:
            w, g_q, g_k, w_o = gqa[ib]
            ib += 1
            qt, k, vt, sg = _prep_call(
                _gqa_prep_kernel, nm + "_gqa_prep", x, (g_pre, w, g_q[None, :], g_k[None, :]),
                _gqa_tables(seq), GQA_HEADS, GQA_KV_HEADS, (), GQA_HEAD_DIM, GQA_WIDTH)
            o = _attention(qt, k, vt, heads_per_step=2, dv=GQA_HEAD_DIM, name=nm + "_gqa_attn")
        else:
            w, lam_vecs, g_sub, w_o = dif[ic]
            ic += 1
            lam_init = 0.8 - 0.6 * math.exp(-0.3 * layer)
            qt, k, vt, sg = _prep_call(
                _diff_prep_kernel, nm + "_dif_prep", x, (g_pre, w), (),
                DIFF_HEADS, DIFF_HEADS, (2,), 2 * DIFF_HEAD_DIM, DIFF_WIDTH)
            o = _diff_attention(qt, k, vt, bias_tiles, lam_vecs, g_sub[:, None],
                                lam_init=lam_init, name=nm + "_dif_attn")
        x = _out_call(x, o, sg, w_o, g_post, nm + "_out")
    return x


def kernel(x_prompt, x_sample, norm_pre, norm_post, rel_bias, mla_w_in, mla_g_q, mla_w_uq, mla_g_kv, mla_w_ukv, mla_w_o, gqa_w_in, gqa_g_q, gqa_g_k, gqa_w_o, dif_w_in, dif_lam_q1, dif_lam_k1, dif_lam_q2, dif_lam_k2, dif_g_sub, dif_w_o):
    mla = []
    for i in range(mla_w_in.shape[0]):
        mla.append(_mla_weights(mla_w_in[i], mla_w_uq[i], mla_w_ukv[i])
                   + (mla_g_q[i], mla_g_kv[i], mla_w_o[i].astype(BF16)))
    gqa = [(gqa_w_in[i].astype(BF16), gqa_g_q[i], gqa_g_k[i], gqa_w_o[i].astype(BF16))
           for i in range(gqa_w_in.shape[0])]
    dif = [(dif_w_in[i].astype(BF16),
            jnp.stack([dif_lam_q1[i], dif_lam_k1[i], dif_lam_q2[i], dif_lam_k2[i]]),
            dif_g_sub[i], dif_w_o[i].astype(BF16))
           for i in range(dif_w_in.shape[0])]
    bias_tiles = _bias_tiles(rel_bias, KEY_TILE, TOKEN_TILE)
    y_prompt = _trunk(x_prompt, "p", norm_pre, norm_post, bias_tiles, mla, gqa, dif)
    y_sample = _trunk(x_sample, "s", norm_pre, norm_post, bias_tiles, mla, gqa, dif)
    return (y_prompt, y_sample)
```
